```python
import jax, jax.numpy as jnp
from jax import lax
import numpy as np

D_MODEL = 1024
BATCH = 16
SEQ = 2048
DEPTH = 1

HEAD_DIM = 64
N_HEADS_SB = 8
N_HEADS_DSA = 8
N_IDX_HEADS = 8
IDX_DIM = 64
TOPK_MAX = 256
D_FF = 2816
PLE_DIM = 256
ROPE_THETA = 500000.0
ROPE_DIM = HEAD_DIM // 4
Q_BLOCK = 128
EPS = 1e-6

W_SB = N_HEADS_SB * HEAD_DIM
W_DSA = N_HEADS_DSA * HEAD_DIM
SPLIT_SIZES = (W_SB, W_SB, W_SB,
               W_DSA, HEAD_DIM, HEAD_DIM,
               N_IDX_HEADS * IDX_DIM, IDX_DIM, N_IDX_HEADS,
               D_MODEL, D_MODEL)
D_IN = 3 * W_SB + W_DSA + 2 * HEAD_DIM + N_IDX_HEADS * IDX_DIM + IDX_DIM + N_IDX_HEADS + 2 * D_MODEL

kernel_name = "hybrid_stickbreak_dsa_macaron_block"


def rmsnorm(x, g):
    xf = x.astype(jnp.float32)
    y = xf * lax.rsqrt(jnp.mean(xf * xf, axis=-1, keepdims=True) + EPS)
    return (y * g.astype(jnp.float32)).astype(x.dtype)


def partial_rotary(x, positions):
    half = ROPE_DIM // 2
    inv_freq = ROPE_THETA ** (-jnp.arange(0, ROPE_DIM, 2, dtype=jnp.float32) / ROPE_DIM)
    ang = positions.astype(jnp.float32)[..., None] * inv_freq
    extra = x.ndim - 3
    ang = ang.reshape(ang.shape[:2] + (1,) * extra + (half,))
    cos, sin = jnp.cos(ang), jnp.sin(ang)
    xf = x.astype(jnp.float32)
    x1, x2, rest = xf[..., :half], xf[..., half:ROPE_DIM], xf[..., ROPE_DIM:]
    out = jnp.concatenate([x1 * cos - x2 * sin, x2 * cos + x1 * sin, rest], axis=-1)
    return out.astype(x.dtype)


def swiglu(x, w1, w2):
    a, b = jnp.split(x @ w1, 2, axis=-1)
    return (jax.nn.silu(a) * b) @ w2


def to_blocks(a):
    B, S = a.shape[:2]
    return jnp.moveaxis(a.reshape((B, S // Q_BLOCK, Q_BLOCK) + a.shape[2:]), 1, 0)


def from_blocks(a):
    a = jnp.moveaxis(a, 0, 1)
    return a.reshape((a.shape[0], a.shape[1] * a.shape[2]) + a.shape[3:])


def stick_breaking_attention(q, k, v):
    S, d = q.shape[1], q.shape[3]
    nb = S // Q_BLOCK
    key_idx = jnp.arange(S)

    def block(args):
        blk, qb = args
        t = blk * Q_BLOCK + jnp.arange(Q_BLOCK)
        causal = (key_idx[None, :] < t[:, None])[None, None]
        z = jnp.einsum('bqhd,bkhd->bhqk', qb, k).astype(jnp.float32) * (d ** -0.5)
        log_keep = jnp.where(causal, jax.nn.log_sigmoid(-z), 0.0)
        log_prefix = lax.cumsum(log_keep, axis=3, reverse=True) - log_keep
        a = jnp.where(causal, jnp.exp(jax.nn.log_sigmoid(z) + log_prefix), 0.0)
        return jnp.einsum('bhqk,bkhd->bqhd', a.astype(v.dtype), v)

    out = lax.map(block, (jnp.arange(nb), to_blocks(q)))
    return from_blocks(out)


def dsa_attention(q, k, v, q_idx, k_idx, w_idx):
    S, d = q.shape[1], q.shape[3]
    nb = S // Q_BLOCK
    n_sel = min(TOPK_MAX, S // 4)
    key_idx = jnp.arange(S)
    gather = jax.vmap(lambda src, ids: src[ids])

    def block(args):
        blk, qb, qib, wib = args
        t = blk * Q_BLOCK + jnp.arange(Q_BLOCK)
        causal = key_idx[None, :] <= t[:, None]
        dots = jnp.einsum('bqhd,bkd->bqhk', qib, k_idx).astype(jnp.float32) * (IDX_DIM ** -0.5)
        score = jnp.einsum('bqh,bqhk->bqk',
                           wib.astype(jnp.float32) * (N_IDX_HEADS ** -0.5), jax.nn.relu(dots))
        score = jnp.where(causal[None], score, -jnp.inf)
        _, sel = lax.top_k(score, n_sel)
        valid = sel <= t[None, :, None]
        k_sel = gather(k, sel)
        v_sel = gather(v, sel)
        logits = jnp.einsum('bqhd,bqnd->bqhn', qb, k_sel).astype(jnp.float32) * (d ** -0.5)
        logits = jnp.where(valid[:, :, None, :], logits, -jnp.inf)
        probs = jax.nn.softmax(logits, axis=-1)
        return jnp.einsum('bqhn,bqnd->bqhd', probs.astype(v.dtype), v_sel)

    out = lax.map(block, (jnp.arange(nb), to_blocks(q), to_blocks(q_idx), to_blocks(w_idx)))
    return from_blocks(out)


def split_columns(c):
    parts, off = [], 0
    for size in SPLIT_SIZES:
        parts.append(c[..., off:off + size])
        off += size
    return parts


def setup_inputs(seed: int = 0) -> dict:
    key = jax.random.key(seed)
    ks = jax.random.split(key, 20)
    f32 = jnp.float32

    def w(k, shape, fan_in):
        return jax.random.normal(k, shape, f32) * (fan_in ** -0.5)

    def gain(k, shape):
        return 1.0 + 0.01 * jax.random.normal(k, shape, f32)

    return {
        "x": jax.random.normal(ks[0], (BATCH, SEQ, D_MODEL), f32),
        "p": jax.random.normal(ks[1], (DEPTH, BATCH, SEQ, PLE_DIM), f32),
        "positions": jnp.broadcast_to(jnp.arange(SEQ, dtype=jnp.int32), (BATCH, SEQ)),
        "ffn1_norm": gain(ks[2], (DEPTH, D_MODEL)),
        "ffn1_w1": w(ks[3], (DEPTH, D_MODEL, 2 * D_FF), D_MODEL),
        "ffn1_w2": w(ks[4], (DEPTH, D_FF, D_MODEL), D_FF),
        "mix_norm": gain(ks[5], (DEPTH, D_MODEL)),
        "w_in": w(ks[6], (DEPTH, D_MODEL, D_IN), D_MODEL),
        "w_out_sb": w(ks[7], (DEPTH, W_SB, D_MODEL), W_SB),
        "w_out_dsa": w(ks[8], (DEPTH, W_DSA, D_MODEL), W_DSA),
        "w_out": w(ks[9], (DEPTH, D_MODEL, D_MODEL), D_MODEL),
        "ffn2_norm": gain(ks[10], (DEPTH, D_MODEL)),
        "ffn2_w1": w(ks[11], (DEPTH, D_MODEL, 2 * D_FF), D_MODEL),
        "ffn2_w2": w(ks[12], (DEPTH, D_FF, D_MODEL), D_FF),
        "ple_norm": gain(ks[13], (DEPTH, D_MODEL)),
        "ple_w_gate": w(ks[14], (DEPTH, D_MODEL, D_MODEL), D_MODEL),
        "ple_w_proj": w(ks[15], (DEPTH, PLE_DIM, D_MODEL), PLE_DIM),
        "final_norm": gain(ks[16], (D_MODEL,)),
    }


def reference(x, p, positions, ffn1_norm, ffn1_w1, ffn1_w2, mix_norm, w_in, w_out_sb,
              w_out_dsa, w_out, ffn2_norm, ffn2_w1, ffn2_w2, ple_norm, ple_w_gate,
              ple_w_proj, final_norm):
    B, S, _ = x.shape
    h = x
    for i in range(DEPTH):
        h = h + 0.5 * swiglu(rmsnorm(h, ffn1_norm[i]), ffn1_w1[i], ffn1_w2[i])

        u = rmsnorm(h, mix_norm[i])
        (q_sb, k_sb, v_sb, q_d, k_d, v_d, q_i, k_i, w_i, g_sb, g_dsa) = split_columns(u @ w_in[i])

        y_sb = stick_breaking_attention(q_sb.reshape(B, S, N_HEADS_SB, HEAD_DIM),
                                        k_sb.reshape(B, S, N_HEADS_SB, HEAD_DIM),
                                        v_sb.reshape(B, S, N_HEADS_SB, HEAD_DIM))
        y_sb = y_sb.reshape(B, S, W_SB) @ w_out_sb[i]

        q_d = partial_rotary(q_d.reshape(B, S, N_HEADS_DSA, HEAD_DIM), positions)
        k_d = partial_rotary(k_d, positions)
        q_i = partial_rotary(q_i.reshape(B, S, N_IDX_HEADS, IDX_DIM), positions)
        k_i = partial_rotary(k_i, positions)
        y_dsa = dsa_attention(q_d, k_d, v_d, q_i, k_i, w_i)
        y_dsa = y_dsa.reshape(B, S, W_DSA) @ w_out_dsa[i]

        merged = jax.nn.sigmoid(g_sb) * y_sb + jax.nn.sigmoid(g_dsa) * y_dsa
        h = h + merged @ w_out[i]

        h = h + 0.5 * swiglu(rmsnorm(h, ffn2_norm[i]), ffn2_w1[i], ffn2_w2[i])

        ple_gate = jax.nn.sigmoid(rmsnorm(h, ple_norm[i]) @ ple_w_gate[i])
        h = h + ple_gate * (p[i] @ ple_w_proj[i])
    return rmsnorm(h, final_norm)
```

```python
import functools

import numpy as np
import jax
import jax.numpy as jnp
from jax import lax
from jax.experimental import pallas as pl
from jax.experimental.pallas import tpu as pltpu

F32 = jnp.float32
BF16 = jnp.bfloat16
I32 = jnp.int32

HEAD_DIM = 64
N_HEADS = 8
W_BRANCH = N_HEADS * HEAD_DIM
TOPK_MAX = 256
ROPE_THETA = 500000.0
ROPE_DIM = HEAD_DIM // 4
EPS = 1e-6

LANES = 128
SUBLANES = 8
ATT_BLOCK = 128
SEARCH_BLOCKS = 4
INT_MIN = np.int32(-2**31)
VMEM_LIMIT = 56 * 1024 * 1024

_NT = (((1,), (1,)), ((), ()))


def _rmsnorm(x, g):
    ms = jnp.mean(x * x, axis=-1, keepdims=True)
    return x * lax.rsqrt(ms + EPS) * g


def _token_tile(n):
    for tm in (512, 256, 128):
        if n % tm == 0:
            return tm
    raise ValueError(f"token count {n} must be a multiple of 128")


def _ff_chunks(d_ff):
    assert d_ff % 256 == 0, d_ff
    chunks, left = [], d_ff
    while left:
        c = min(768, left)
        chunks.append(c)
        left -= c
    return tuple(chunks)


def _params(n_axes):
    return pltpu.CompilerParams(dimension_semantics=("arbitrary",) * n_axes,
                                vmem_limit_bytes=VMEM_LIMIT)


def _full(shape):
    nd = len(shape)
    return pl.BlockSpec(shape, lambda *_: (0,) * nd)


def _ffn_kernel(x_ref, g_ref, w1a_ref, w1b_ref, w2_ref, o_ref, *, chunks):
    x = x_ref[...]
    xn = _rmsnorm(x, g_ref[...]).astype(BF16)
    acc = jnp.zeros(x.shape, F32)
    off = 0
    for cw in chunks:
        a = jnp.dot(xn, w1a_ref[:, off:off + cw], preferred_element_type=F32)
        b = jnp.dot(xn, w1b_ref[:, off:off + cw], preferred_element_type=F32)
        hid = (a * jax.nn.sigmoid(a) * b).astype(BF16)
        acc = acc + jnp.dot(hid, w2_ref[off:off + cw, :], preferred_element_type=F32)
        off += cw
    o_ref[...] = x + 0.5 * acc


def _ffn_half_step(xf, norm, w1, w2):
    n, d = xf.shape
    d_ff = w2.shape[0]
    tm = _token_tile(n)
    w1a = w1[:, :d_ff].astype(BF16)
    w1b = w1[:, d_ff:].astype(BF16)
    w2b = w2.astype(BF16)
    tok = pl.BlockSpec((tm, d), lambda i: (i, 0))
    return pl.pallas_call(
        functools.partial(_ffn_kernel, chunks=_ff_chunks(d_ff)),
        grid=(n // tm,),
        in_specs=[tok, _full((1, d)), _full(w1a.shape), _full(w1b.shape), _full(w2b.shape)],
        out_specs=tok,
        out_shape=jax.ShapeDtypeStruct((n, d), F32),
        compiler_params=_params(1),
        name="ffn_half_step",
    )(xf, norm.reshape(1, d), w1a, w1b, w2b)


def _proj_kernel(h_ref, g_ref, pos_ref, invf_ref, wsb_ref, wrot_ref, wt_ref,
                 sb_ref, qd_ref, qi_ref, kd_ref, ki_ref, vt_ref, wgt_ref):
    u = _rmsnorm(h_ref[...], g_ref[...]).astype(BF16)
    sb_ref[...] = jnp.dot(u, wsb_ref[...], preferred_element_type=F32).astype(BF16)

    ang = pos_ref[...].astype(F32) * invf_ref[...]
    cos = jnp.cos(ang)
    sin = jnp.sin(ang)
    lane = lax.broadcasted_iota(I32, (1, LANES), 1) % HEAD_DIM
    half = ROPE_DIM // 2
    s_lo = jnp.where(lane < half, -sin, 0.0)
    s_hi = jnp.where((lane >= half) & (lane < ROPE_DIM), sin, 0.0)
    rot = jnp.dot(u, wrot_ref[...], preferred_element_type=F32)
    outs = ((qd_ref, 0, W_BRANCH), (qi_ref, W_BRANCH, W_BRANCH),
            (kd_ref, 2 * W_BRANCH, 2 * LANES), (ki_ref, 2 * W_BRANCH + 2 * LANES, 2 * LANES))
    for ref, base, width in outs:
        for g in range(width // LANES):
            xg = rot[:, base + g * LANES: base + (g + 1) * LANES]
            yg = (xg * cos + pltpu.roll(xg, LANES - half, 1) * s_lo
                  + pltpu.roll(xg, half, 1) * s_hi)
            ref[:, g * LANES:(g + 1) * LANES] = yg.astype(BF16)

    tr = lax.dot_general(wt_ref[...], u, _NT, preferred_element_type=F32)
    vt_ref[...] = tr[:HEAD_DIM].astype(BF16)
    wgt_ref[...] = tr[HEAD_DIM:] * (N_HEADS ** -0.5)


def _mixer_proj(hf, positions, norm, w_in):
    n, d = hf.shape
    tm = _token_tile(n)
    wb = W_BRANCH
    o = 0
    cols = {}
    for name, size in (("q_sb", wb), ("k_sb", wb), ("v_sb", wb), ("q_d", wb), ("k_d", HEAD_DIM),
                       ("v_d", HEAD_DIM), ("q_i", wb), ("k_i", HEAD_DIM), ("w_i", N_HEADS),
                       ("g_sb", d), ("g_dsa", d)):
        cols[name] = w_in[:, o:o + size]
        o += size
    scale = HEAD_DIM ** -0.5
    zeros = jnp.zeros((d, HEAD_DIM), w_in.dtype)
    wsb = jnp.concatenate([cols["q_sb"] * scale, cols["k_sb"], cols["v_sb"]], axis=1).astype(BF16)
    wrot = jnp.concatenate([cols["q_d"] * scale, cols["q_i"] * scale,
                            cols["k_d"], zeros, zeros, cols["k_d"],
                            cols["k_i"], zeros, zeros, cols["k_i"]], axis=1).astype(BF16)
    wt = jnp.concatenate([cols["v_d"], cols["w_i"]], axis=1).T.astype(BF16)
    wgate = jnp.concatenate([cols["g_sb"], cols["g_dsa"]], axis=1).astype(BF16)

    lane = np.arange(LANES) % HEAD_DIM
    inv_freq = ROPE_THETA ** (-jnp.arange(0, ROPE_DIM, 2, dtype=F32) / ROPE_DIM)
    invf = jnp.where(lane < ROPE_DIM, inv_freq[lane % (ROPE_DIM // 2)], 0.0).reshape(1, LANES)

    def tok(w):
        return pl.BlockSpec((tm, w), lambda i: (i, 0))

    def tok_t(r):
        return pl.BlockSpec((r, tm), lambda i: (0, i))

    outs = pl.pallas_call(
        _proj_kernel,
        grid=(n // tm,),
        in_specs=[tok(d), _full((1, d)), tok(1), _full((1, LANES)),
                  _full(wsb.shape), _full(wrot.shape), _full(wt.shape)],
        out_specs=[tok(3 * wb), tok(wb), tok(wb), tok(2 * LANES), tok(2 * LANES),
                   tok_t(HEAD_DIM), tok_t(N_HEADS)],
        out_shape=[jax.ShapeDtypeStruct((n, 3 * wb), BF16),
                   jax.ShapeDtypeStruct((n, wb), BF16),
                   jax.ShapeDtypeStruct((n, wb), BF16),
                   jax.ShapeDtypeStruct((n, 2 * LANES), BF16),
                   jax.ShapeDtypeStruct((n, 2 * LANES), BF16),
                   jax.ShapeDtypeStruct((HEAD_DIM, n), BF16),
                   jax.ShapeDtypeStruct((N_HEADS, n), F32)],
        compiler_params=_params(1),
        name="mixer_proj",
    )(hf, norm.reshape(1, d), positions.reshape(n, 1), invf, wsb, wrot, wt)
    return outs, wgate


def _sb_kernel(q_ref, k_ref, v_ref, o_ref, acc_ref, car_ref, *, tb):
    i = pl.program_id(2)
    q = q_ref[0]
    lane = lax.broadcasted_iota(I32, (tb, LANES), 1)
    qh = (jnp.where(lane < HEAD_DIM, q, jnp.zeros_like(q)),
          jnp.where(lane >= HEAD_DIM, q, jnp.zeros_like(q)))
    r = lax.broadcasted_iota(I32, (2 * tb, 2 * tb), 0) % tb
    c = lax.broadcasted_iota(I32, (2 * tb, 2 * tb), 1)
    suffix = jnp.where((c >= tb) | (r > c), 1.0, 0.0).astype(BF16)
    row = lax.broadcasted_iota(I32, (tb, tb), 0)
    col = lax.broadcasted_iota(I32, (tb, tb), 1)
    strict = col < row

    acc_ref[...] = jnp.zeros(acc_ref.shape, F32)
    car_ref[...] = jnp.zeros(car_ref.shape, F32)

    def tile(j, diag):
        off = pl.multiple_of(j * tb, tb)
        kj = k_ref[0, pl.ds(off, tb), :]
        vj = v_ref[0, pl.ds(off, tb), :]
        for h in range(2):
            z = lax.dot_general(qh[h], kj, _NT, preferred_element_type=F32)
            log_keep = jnp.minimum(-z, 0.0) - jnp.log(1.0 + jnp.exp(-jnp.abs(z)))
            if diag:
                log_keep = jnp.where(strict, log_keep, 0.0)
            hi = log_keep.astype(BF16)
            lo = (log_keep - hi.astype(F32)).astype(BF16)
            sums = jnp.dot(jnp.concatenate([hi, lo], axis=1), suffix,
                           preferred_element_type=F32)
            a = jnp.exp(z + log_keep + sums[:, :tb] + car_ref[h])
            if diag:
                a = jnp.where(strict, a, 0.0)
            acc_ref[h] += jnp.dot(a.astype(BF16), vj, preferred_element_type=F32)
            car_ref[h] += sums[:, tb:]

    tile(i, True)

    def body(jj, carry):
        tile(i - 1 - jj, False)
        return carry

    lax.fori_loop(0, i, body, 0)
    o_ref[0] = jnp.where(lane < HEAD_DIM, acc_ref[0], acc_ref[1]).astype(BF16)


def _sb_attention(qkv, b, s):
    tb = ATT_BLOCK
    pairs = W_BRANCH // LANES
    return pl.pallas_call(
        functools.partial(_sb_kernel, tb=tb),
        grid=(b, pairs, s // tb),
        in_specs=[pl.BlockSpec((1, tb, LANES), lambda bi, g, i: (bi, i, g)),
                  pl.BlockSpec((1, s, LANES), lambda bi, g, i: (bi, 0, pairs + g)),
                  pl.BlockSpec((1, s, LANES), lambda bi, g, i: (bi, 0, 2 * pairs + g))],
        out_specs=pl.BlockSpec((1, tb, LANES), lambda bi, g, i: (bi, i, g)),
        out_shape=jax.ShapeDtypeStruct((b, s, W_BRANCH), BF16),
        scratch_shapes=[pltpu.VMEM((2, tb, LANES), F32), pltpu.VMEM((2, tb, LANES), F32)],
        compiler_params=_params(3),
        name="sb_attention",
    )(qkv, qkv, qkv)


def _dsa_kernel(qi_ref, qd_ref, ki_ref, kd_ref, vt_ref, wgt_ref, o_ref,
                key_ref, bias_ref, lg_ref, *, tb, n_sel):
    i = pl.program_id(1)
    nblk = i + 1
    span = SEARCH_BLOCKS * tb
    nspan = (nblk + SEARCH_BLOCKS - 1) // SEARCH_BLOCKS
    row = lax.broadcasted_iota(I32, (tb, tb), 0)
    col = lax.broadcasted_iota(I32, (tb, tb), 1)

    def blk(j):
        return pl.ds(pl.multiple_of(j * tb, tb), tb)

    def causal(j):
        return (j * tb + row) <= (i * tb + col)

    def fold(x, op):
        return op(x.reshape(x.shape[0] // SUBLANES, SUBLANES, x.shape[1]), axis=0)

    def pair_lhs(ref, j):
        kab = ref[0, blk(j), :]
        return jnp.concatenate([kab[:, :LANES], kab[:, LANES:]], axis=0)

    def score_block(j, carry):
        lhs = pair_lhs(ki_ref, j)
        score = jnp.zeros((tb, tb), F32)
        for g in range(N_HEADS // 2):
            zz = lax.dot_general(lhs, qi_ref[0, :, g * LANES:(g + 1) * LANES], _NT,
                                 preferred_element_type=F32)
            score = score + wgt_ref[2 * g:2 * g + 1, :] * jnp.maximum(zz[:tb], 0.0)
            score = score + wgt_ref[2 * g + 1:2 * g + 2, :] * jnp.maximum(zz[tb:], 0.0)
        bits = lax.bitcast_convert_type(score, I32)
        bits = jnp.where(bits == INT_MIN, 0, bits)
        key = bits ^ ((bits >> 31) & np.int32(0x7FFFFFFF))
        key_ref[blk(j), :] = jnp.where(causal(j), key, INT_MIN)
        return carry

    lax.fori_loop(0, nblk, score_block, 0)

    def pad_block(j, carry):
        key_ref[blk(j), :] = jnp.full((tb, tb), INT_MIN, I32)
        return carry

    lax.fori_loop(nblk, nspan * SEARCH_BLOCKS, pad_block, 0)

    def count_ge(cand):
        def step(sidx, acc):
            kk = key_ref[pl.ds(pl.multiple_of(sidx * span, span), span), :]
            return acc + fold(jnp.where(kk >= cand, 1, 0).astype(I32), jnp.sum)
        acc = lax.fori_loop(0, nspan, step, jnp.zeros((SUBLANES, tb), I32))
        return jnp.sum(acc, axis=0, keepdims=True)

    def refine(cand, state):
        thr, cnt = state
        c = count_ge(cand)
        ok = c >= n_sel
        return jnp.where(ok, cand, thr), jnp.where(ok, c, cnt)

    state = (jnp.full((1, tb), INT_MIN, I32), jnp.zeros((1, tb), I32) + nspan * span)
    state = refine(jnp.zeros((1, tb), I32), state)

    def bit_pass(it, state):
        return refine(state[0] + jnp.left_shift(np.int32(1), 30 - it), state)

    thr, cnt = lax.fori_loop(0, 31, bit_pass, state)

    def plain_bias(j, carry):
        sel = (key_ref[blk(j), :] >= thr) & causal(j)
        bias_ref[blk(j), :] = jnp.where(sel, 0.0, -jnp.inf)
        return carry

    lax.fori_loop(0, nblk, plain_bias, 0)

    tie = (cnt > n_sel) & (thr > INT_MIN)

    @pl.when(jnp.max(tie.astype(I32)) > 0)
    def _():
        def count_gt(j, acc):
            return acc + fold(jnp.where(key_ref[blk(j), :] > thr, 1, 0).astype(I32), jnp.sum)
        n_gt = jnp.sum(lax.fori_loop(0, nblk, count_gt, jnp.zeros((SUBLANES, tb), I32)),
                       axis=0, keepdims=True)
        need = jnp.where(tie, (n_sel - n_gt).astype(F32), 3.0e38)
        before = jnp.where(col < row, 1.0, 0.0).astype(BF16)

        def tie_bias(j, seen):
            kk = key_ref[blk(j), :]
            eq = kk == thr
            eqf = jnp.where(eq, 1.0, 0.0)
            rank = jnp.dot(before, eqf.astype(BF16), preferred_element_type=F32) + seen
            sel = ((kk > thr) | (eq & (rank < need))) & causal(j)
            bias_ref[blk(j), :] = jnp.where(sel, 0.0, -jnp.inf)
            return seen + jnp.sum(eqf, axis=0, keepdims=True)

        lax.fori_loop(0, nblk, tie_bias, jnp.zeros((1, tb), F32))

    for g in range(N_HEADS // 2):
        qd = qd_ref[0, :, g * LANES:(g + 1) * LANES]

        def logits_block(j, mx):
            zz = lax.dot_general(pair_lhs(kd_ref, j), qd, _NT, preferred_element_type=F32)
            bias = bias_ref[blk(j), :]
            l0 = zz[:tb] + bias
            l1 = zz[tb:] + bias
            lg_ref[0, blk(j), :] = l0
            lg_ref[1, blk(j), :] = l1
            return (jnp.maximum(mx[0], fold(l0, jnp.max)), jnp.maximum(mx[1], fold(l1, jnp.max)))

        ninf = jnp.full((SUBLANES, tb), -jnp.inf, F32)
        mx = lax.fori_loop(0, nblk, logits_block, (ninf, ninf))
        mx = tuple(jnp.max(m, axis=0, keepdims=True) for m in mx)

        def pv_block(j, carry):
            vt = vt_ref[0, j]
            new = []
            for h in range(2):
                den, out = carry[h]
                p = jnp.exp(lg_ref[h, blk(j), :] - mx[h])
                out = out + jnp.dot(vt, p.astype(BF16), preferred_element_type=F32)
                new.append((den + fold(p, jnp.sum), out))
            return tuple(new)

        zero = (jnp.zeros((SUBLANES, tb), F32), jnp.zeros((HEAD_DIM, tb), F32))
        res = lax.fori_loop(0, nblk, pv_block, (zero, zero))
        for h in range(2):
            den, out = res[h]
            out = out / jnp.sum(den, axis=0, keepdims=True)
            o_ref[0, (2 * g + h) * HEAD_DIM:(2 * g + h + 1) * HEAD_DIM, :] = out.astype(BF16)


def _dsa_attention(qd, qi, kd, ki, vt, wgt, b, s):
    tb = ATT_BLOCK
    nq = s // tb
    assert nq % SEARCH_BLOCKS == 0, s
    n_sel = min(TOPK_MAX, s // 4)
    return pl.pallas_call(
        functools.partial(_dsa_kernel, tb=tb, n_sel=n_sel),
        grid=(b, nq),
        in_specs=[pl.BlockSpec((1, tb, W_BRANCH), lambda bi, i: (bi, i, 0)),
                  pl.BlockSpec((1, tb, W_BRANCH), lambda bi, i: (bi, i, 0)),
                  pl.BlockSpec((1, s, 2 * LANES), lambda bi, i: (bi, 0, 0)),
                  pl.BlockSpec((1, s, 2 * LANES), lambda bi, i: (bi, 0, 0)),
                  pl.BlockSpec((1, nq, HEAD_DIM, tb), lambda bi, i: (bi, 0, 0, 0)),
                  pl.BlockSpec((N_HEADS, tb), lambda bi, i: (0, bi * nq + i))],
        out_specs=pl.BlockSpec((1, W_BRANCH, tb), lambda bi, i: (bi, 0, i)),
        out_shape=jax.ShapeDtypeStruct((b, W_BRANCH, s), BF16),
        scratch_shapes=[pltpu.VMEM((s, tb), I32), pltpu.VMEM((s, tb), F32),
                        pltpu.VMEM((2, s, tb), F32)],
        compiler_params=_params(2),
        name="dsa_attention",
    )(qi, qd, ki, kd, vt, wgt)


def _merge_kernel(h_ref, g_ref, ysb_ref, yds_ref, wg_ref, wosb_ref, wods_ref, wo_ref, o_ref):
    h = h_ref[...]
    d = h.shape[1]
    u = _rmsnorm(h, g_ref[...]).astype(BF16)
    gates = jax.nn.sigmoid(jnp.dot(u, wg_ref[...], preferred_element_type=F32))
    y_sb = jnp.dot(ysb_ref[...], wosb_ref[...], preferred_element_type=F32)
    y_ds = jnp.dot(yds_ref[...], wods_ref[...], preferred_element_type=F32)
    merged = gates[:, :d] * y_sb + gates[:, d:] * y_ds
    o_ref[...] = h + jnp.dot(merged.astype(BF16), wo_ref[...], preferred_element_type=F32)


def _merge_out(hf, norm, y_sb, y_dsa, wgate, w_out_sb, w_out_dsa, w_out):
    n, d = hf.shape
    tm = _token_tile(n)
    wb = W_BRANCH

    def tok(w):
        return pl.BlockSpec((tm, w), lambda i: (i, 0))

    return pl.pallas_call(
        _merge_kernel,
        grid=(n // tm,),
        in_specs=[tok(d), _full((1, d)), tok(wb), tok(wb), _full(wgate.shape),
                  _full((wb, d)), _full((wb, d)), _full((d, d))],
        out_specs=tok(d),
        out_shape=jax.ShapeDtypeStruct((n, d), F32),
        compiler_params=_params(1),
        name="merge_out",
    )(hf, norm.reshape(1, d), y_sb, y_dsa, wgate,
      w_out_sb.astype(BF16), w_out_dsa.astype(BF16), w_out.astype(BF16))


def _ple_kernel(h_ref, g_ref, p_ref, wg_ref, wp_ref, gf_ref, o_ref, *, final):
    h = h_ref[...]
    u = _rmsnorm(h, g_ref[...]).astype(BF16)
    gate = jax.nn.sigmoid(jnp.dot(u, wg_ref[...], preferred_element_type=F32))
    emb = jnp.dot(p_ref[...].astype(BF16), wp_ref[...], preferred_element_type=F32)
    h = h + gate * emb
    o_ref[...] = _rmsnorm(h, gf_ref[...]) if final else h


def _ple_final(hf, norm, pf, w_gate, w_proj, final_norm, final):
    n, d = hf.shape
    tm = _token_tile(n)
    dp = pf.shape[1]

    def tok(w):
        return pl.BlockSpec((tm, w), lambda i: (i, 0))

    return pl.pallas_call(
        functools.partial(_ple_kernel, final=final),
        grid=(n // tm,),
        in_specs=[tok(d), _full((1, d)), tok(dp), _full((d, d)), _full((dp, d)), _full((1, d))],
        out_specs=tok(d),
        out_shape=jax.ShapeDtypeStruct((n, d), F32),
        compiler_params=_params(1),
        name="ple_final",
    )(hf, norm.reshape(1, d), pf, w_gate.astype(BF16), w_proj.astype(BF16),
      final_norm.reshape(1, d))


def kernel(x, p, positions, ffn1_norm, ffn1_w1, ffn1_w2, mix_norm, w_in, w_out_sb, w_out_dsa,
           w_out, ffn2_norm, ffn2_w1, ffn2_w2, ple_norm, ple_w_gate, ple_w_proj, final_norm):
    b, s, d = x.shape
    n = b * s
    depth = p.shape[0]
    nq = s // ATT_BLOCK
    h = x.reshape(n, d)
    for i in range(depth):
        h = _ffn_half_step(h, ffn1_norm[i], ffn1_w1[i], ffn1_w2[i])
        (sb, qd, qi, kd, ki, vt, wgt), wgate = _mixer_proj(h, positions, mix_norm[i], w_in[i])
        y_sb = _sb_attention(sb.reshape(b, s, 3 * W_BRANCH), b, s)
        vt4 = vt.reshape(HEAD_DIM, b, nq, ATT_BLOCK).transpose(1, 2, 0, 3)
        y_dsa_t = _dsa_attention(qd.reshape(b, s, W_BRANCH), qi.reshape(b, s, W_BRANCH),
                                 kd.reshape(b, s, 2 * LANES), ki.reshape(b, s, 2 * LANES),
                                 vt4, wgt, b, s)
        y_dsa = y_dsa_t.transpose(0, 2, 1).reshape(n, W_BRANCH)
        h = _merge_out(h, mix_norm[i], y_sb.reshape(n, W_BRANCH), y_dsa, wgate,
                       w_out_sb[i], w_out_dsa[i], w_out[i])
        h = _ffn_half_step(h, ffn2_norm[i], ffn2_w1[i], ffn2_w2[i])
        h = _ple_final(h, ple_norm[i], p[i].reshape(n, -1), ple_w_gate[i], ple_w_proj[i],
                       final_norm, final=(i == depth - 1))
    return h.reshape(b, s, d)
```

```python
import functools

import numpy as np
import jax
import jax.numpy as jnp
from jax import lax
from jax.experimental import pallas as pl
from jax.experimental.pallas import tpu as pltpu

F32 = jnp.float32
BF16 = jnp.bfloat16
I32 = jnp.int32

HEAD_DIM = 64
N_HEADS = 8
W_BRANCH = N_HEADS * HEAD_DIM
TOPK_MAX = 256
ROPE_THETA = 500000.0
ROPE_DIM = HEAD_DIM // 4
EPS = 1e-6

LANES = 128
SUBLANES = 8
ATT_BLOCK = 128
SEARCH_BLOCKS = 4
INT_MIN = np.int32(-2**31)
VMEM_LIMIT = 56 * 1024 * 1024

_NT = (((1,), (1,)), ((), ()))


def _rmsnorm(x, g):
    ms = jnp.mean(x * x, axis=-1, keepdims=True)
    return x * lax.rsqrt(ms + EPS) * g


def _token_tile(n):
    for tm in (512, 256, 128):
        if n % tm == 0:
            return tm
    raise ValueError(f"token count {n} must be a multiple of 128")


def _ff_chunks(d_ff):
    assert d_ff % 256 == 0, d_ff
    chunks, left = [], d_ff
    while left:
        c = min(768, left)
        chunks.append(c)
        left -= c
    return tuple(chunks)


def _params(n_axes):
    return pltpu.CompilerParams(dimension_semantics=("arbitrary",) * n_axes,
                                vmem_limit_bytes=VMEM_LIMIT)


def _full(shape):
    nd = len(shape)
    return pl.BlockSpec(shape, lambda *_: (0,) * nd)


def _ffn_kernel(x_ref, g_ref, w1a_ref, w1b_ref, w2_ref, o_ref, *, chunks):
    x = x_ref[...]
    xn = _rmsnorm(x, g_ref[...]).astype(BF16)
    acc = jnp.zeros(x.shape, F32)
    off = 0
    for cw in chunks:
        a = jnp.dot(xn, w1a_ref[:, off:off + cw], preferred_element_type=F32)
        b = jnp.dot(xn, w1b_ref[:, off:off + cw], preferred_element_type=F32)
        hid = (a * jax.nn.sigmoid(a) * b).astype(BF16)
        acc = acc + jnp.dot(hid, w2_ref[off:off + cw, :], preferred_element_type=F32)
        off += cw
    o_ref[...] = x + 0.5 * acc


def _ffn_half_step(xf, norm, w1, w2):
    n, d = xf.shape
    d_ff = w2.shape[0]
    tm = _token_tile(n)
    w1a = w1[:, :d_ff].astype(BF16)
    w1b = w1[:, d_ff:].astype(BF16)
    w2b = w2.astype(BF16)
    tok = pl.BlockSpec((tm, d), lambda i: (i, 0))
    return pl.pallas_call(
        functools.partial(_ffn_kernel, chunks=_ff_chunks(d_ff)),
        grid=(n // tm,),
        in_specs=[tok, _full((1, d)), _full(w1a.shape), _full(w1b.shape), _full(w2b.shape)],
        out_specs=tok,
        out_shape=jax.ShapeDtypeStruct((n, d), F32),
        compiler_params=_params(1),
        name="ffn_half_step",
    )(xf, norm.reshape(1, d), w1a, w1b, w2b)


def _proj_kernel(h_ref, g_ref, pos_ref, invf_ref, wsb_ref, wrot_ref, wt_ref,
                 sb_ref, qd_ref, qi_ref, kd_ref, ki_ref, vt_ref, wgt_ref):
    u = _rmsnorm(h_ref[...], g_ref[...]).astype(BF16)
    sb_ref[...] = jnp.dot(u, wsb_ref[...], preferred_element_type=F32).astype(BF16)

    ang = pos_ref[...].astype(F32) * invf_ref[...]
    cos = jnp.cos(ang)
    sin = jnp.sin(ang)
    lane = lax.broadcasted_iota(I32, (1, LANES), 1) % HEAD_DIM
    half = ROPE_DIM // 2
    s_lo = jnp.where(lane < half, -sin, 0.0)
    s_hi = jnp.where((lane >= half) & (lane < ROPE_DIM), sin, 0.0)
    rot = jnp.dot(u, wrot_ref[...], preferred_element_type=F32)
    outs = ((qd_ref, 0, W_BRANCH), (qi_ref, W_BRANCH, W_BRANCH),
            (kd_ref, 2 * W_BRANCH, 2 * LANES), (ki_ref, 2 * W_BRANCH + 2 * LANES, 2 * LANES))
    for ref, base, width in outs:
        for g in range(width // LANES):
            xg = rot[:, base + g * LANES: base + (g + 1) * LANES]
            yg = (xg * cos + pltpu.roll(xg, LANES - half, 1) * s_lo
                  + pltpu.roll(xg, half, 1) * s_hi)
            ref[:, g * LANES:(g + 1) * LANES] = yg.astype(BF16)

    tr = lax.dot_general(wt_ref[...], u, _NT, preferred_element_type=F32)
    vt_ref[...] = tr[:HEAD_DIM].astype(BF16)
    wgt_ref[...] = tr[HEAD_DIM:] * (N_HEADS ** -0.5)


def _mixer_proj(hf, positions, norm, w_in):
    n, d = hf.shape
    tm = _token_tile(n)
    wb = W_BRANCH
    o = 0
    cols = {}
    for name, size in (("q_sb", wb), ("k_sb", wb), ("v_sb", wb), ("q_d", wb), ("k_d", HEAD_DIM),
                       ("v_d", HEAD_DIM), ("q_i", wb), ("k_i", HEAD_DIM), ("w_i", N_HEADS),
                       ("g_sb", d), ("g_dsa", d)):
        cols[name] = w_in[:, o:o + size]
        o += size
    scale = HEAD_DIM ** -0.5
    zeros = jnp.zeros((d, HEAD_DIM), w_in.dtype)
    wsb = jnp.concatenate([cols["q_sb"] * scale, cols["k_sb"], cols["v_sb"]], axis=1).astype(BF16)
    wrot = jnp.concatenate([cols["q_d"] * scale, cols["q_i"] * scale,
                            cols["k_d"], zeros, zeros, cols["k_d"],
                            cols["k_i"], zeros, zeros, cols["k_i"]], axis=1).astype(BF16)
    wt = jnp.concatenate([cols["v_d"], cols["w_i"]], axis=1).T.astype(BF16)
    wgate = jnp.concatenate([cols["g_sb"], cols["g_dsa"]], axis=1).astype(BF16)

    lane = np.arange(LANES) % HEAD_DIM
    inv_freq = ROPE_THETA ** (-jnp.arange(0, ROPE_DIM, 2, dtype=F32) / ROPE_DIM)
    invf = jnp.where(lane < ROPE_DIM, inv_freq[lane % (ROPE_DIM // 2)], 0.0).reshape(1, LANES)

    def tok(w):
        return pl.BlockSpec((tm, w), lambda i: (i, 0))

    def tok_t(r):
        return pl.BlockSpec((r, tm), lambda i: (0, i))

    outs = pl.pallas_call(
        _proj_kernel,
        grid=(n // tm,),
        in_specs=[tok(d), _full((1, d)), tok(1), _full((1, LANES)),
                  _full(wsb.shape), _full(wrot.shape), _full(wt.shape)],
        out_specs=[tok(3 * wb), tok(wb), tok(wb), tok(2 * LANES), tok(2 * LANES),
                   tok_t(HEAD_DIM), tok_t(N_HEADS)],
        out_shape=[jax.ShapeDtypeStruct((n, 3 * wb), BF16),
                   jax.ShapeDtypeStruct((n, wb), BF16),
                   jax.ShapeDtypeStruct((n, wb), BF16),
                   jax.ShapeDtypeStruct((n, 2 * LANES), BF16),
                   jax.ShapeDtypeStruct((n, 2 * LANES), BF16),
                   jax.ShapeDtypeStruct((HEAD_DIM, n), BF16),
                   jax.ShapeDtypeStruct((N_HEADS, n), F32)],
        compiler_params=_params(1),
        name="mixer_proj",
    )(hf, norm.reshape(1, d), positions.reshape(n, 1), invf, wsb, wrot, wt)
    return outs, wgate


def _sb_kernel(q_ref, k_ref, v_ref, o_ref, qm_ref, acc_ref, car_ref, *, tq, sub):
    i = pl.program_id(1)
    nsub = tq // sub
    pairs = W_BRANCH // LANES
    lane = lax.broadcasted_iota(I32, (tq, LANES), 1)
    lo_half = lane < HEAD_DIM
    q = q_ref[0]
    for g in range(pairs):
        qg = q[:, g * LANES:(g + 1) * LANES]
        qm_ref[2 * g] = jnp.where(lo_half, qg, jnp.zeros_like(qg))
        qm_ref[2 * g + 1] = jnp.where(lo_half, jnp.zeros_like(qg), qg)
    r = lax.broadcasted_iota(I32, (2 * sub, 2 * sub), 0) % sub
    c = lax.broadcasted_iota(I32, (2 * sub, 2 * sub), 1)
    suffix = jnp.where((c >= sub) | (r > c), 1.0, 0.0).astype(BF16)
    strict = (lax.broadcasted_iota(I32, (tq, tq), 1) < lax.broadcasted_iota(I32, (tq, tq), 0))

    acc_ref[...] = jnp.zeros(acc_ref.shape, F32)
    car_ref[...] = jnp.zeros(car_ref.shape, F32)

    def group(j, diag):
        rows = pl.ds(pl.multiple_of(j * tq, tq), tq)
        hl, zl = [], []
        for g in range(pairs):
            kg = k_ref[0, rows, g * LANES:(g + 1) * LANES]
            zz = lax.dot_general(jnp.concatenate([qm_ref[2 * g], qm_ref[2 * g + 1]], axis=0), kg,
                                 _NT, preferred_element_type=F32)
            for hh in range(2):
                z = zz[hh * tq:(hh + 1) * tq]
                sp = jnp.maximum(z, 0.0) + jnp.log(1.0 + jnp.exp(-jnp.abs(z)))
                if diag:
                    sp = jnp.where(strict, sp, 0.0)
                for sb in range(nsub):
                    part = sp[:, sb * sub:(sb + 1) * sub]
                    hi = part.astype(BF16)
                    lo = (part - hi.astype(F32)).astype(BF16)
                    hl.append(jnp.concatenate([hi, lo], axis=1))
                zl.append(z - sp)
        sums = jnp.dot(jnp.concatenate(hl, axis=0), suffix, preferred_element_type=F32)
        for g in range(pairs):
            probs = []
            for hh in range(2):
                h = 2 * g + hh
                run = car_ref[h]
                parts = [None] * nsub
                for sb in reversed(range(nsub)):
                    blk = sums[(h * nsub + sb) * tq:(h * nsub + sb + 1) * tq]
                    a = jnp.exp(zl[h][:, sb * sub:(sb + 1) * sub] - (blk[:, :sub] + run))
                    if diag:
                        a = jnp.where(strict[:, sb * sub:(sb + 1) * sub], a, 0.0)
                    parts[sb] = a.astype(BF16)
                    run = run + blk[:, sub:]
                car_ref[h] = run
                probs.extend(parts)
            vg = v_ref[0, rows, g * LANES:(g + 1) * LANES]
            vv = jnp.concatenate([jnp.where(lo_half, vg, jnp.zeros_like(vg)),
                                  jnp.where(lo_half, jnp.zeros_like(vg), vg)], axis=0)
            acc_ref[g] += jnp.dot(jnp.concatenate(probs, axis=1), vv, preferred_element_type=F32)

    group(i, True)

    def body(jj, carry):
        group(i - 1 - jj, False)
        return carry

    lax.fori_loop(0, i, body, 0)
    for g in range(pairs):
        o_ref[0, :, g * LANES:(g + 1) * LANES] = acc_ref[g].astype(BF16)


def _sb_attention(qkv, b, s):
    sub = ATT_BLOCK
    tq = 2 * sub if s % (2 * sub) == 0 else sub
    wb = W_BRANCH
    return pl.pallas_call(
        functools.partial(_sb_kernel, tq=tq, sub=sub),
        grid=(b, s // tq),
        in_specs=[pl.BlockSpec((1, tq, wb), lambda bi, i: (bi, i, 0)),
                  pl.BlockSpec((1, s, wb), lambda bi, i: (bi, 0, 1)),
                  pl.BlockSpec((1, s, wb), lambda bi, i: (bi, 0, 2))],
        out_specs=pl.BlockSpec((1, tq, wb), lambda bi, i: (bi, i, 0)),
        out_shape=jax.ShapeDtypeStruct((b, s, wb), BF16),
        scratch_shapes=[pltpu.VMEM((N_HEADS, tq, LANES), BF16),
                        pltpu.VMEM((wb // LANES, tq, LANES), F32),
                        pltpu.VMEM((N_HEADS, tq, LANES), F32)],
        compiler_params=_params(2),
        name="sb_attention",
    )(qkv, qkv, qkv)


def _dsa_kernel(qi_ref, qd_ref, ki_ref, kd_ref, vt_ref, wgt_ref, o_ref,
                key_ref, bias_ref, lg_ref, *, tb, n_sel):
    i = pl.program_id(1)
    span = SEARCH_BLOCKS * tb
    nspan = i // SEARCH_BLOCKS + 1
    nblk = nspan * SEARCH_BLOCKS
    row = lax.broadcasted_iota(I32, (tb, tb), 0)
    col = lax.broadcasted_iota(I32, (tb, tb), 1)
    srow = lax.broadcasted_iota(I32, (span, tb), 0)
    scol = lax.broadcasted_iota(I32, (span, tb), 1)

    def blk(j):
        return pl.ds(pl.multiple_of(j * tb, tb), tb)

    def rows(c):
        return pl.ds(pl.multiple_of(c * span, span), span)

    def causal(j):
        return (j * tb + row) <= (i * tb + col)

    def span_causal(c):
        return (c * span + srow) <= (i * tb + scol)

    def fold(x, op):
        return op(x.reshape(x.shape[0] // SUBLANES, SUBLANES, x.shape[1]), axis=0)

    def pair_lhs(ref, c):
        kab = ref[0, rows(c), :]
        return jnp.concatenate([kab[:, :LANES], kab[:, LANES:]], axis=0)

    def score_span(c, carry):
        lhs = pair_lhs(ki_ref, c)
        score = jnp.zeros((span, tb), F32)
        for g in range(N_HEADS // 2):
            zz = lax.dot_general(lhs, qi_ref[0, :, g * LANES:(g + 1) * LANES], _NT,
                                 preferred_element_type=F32)
            score = score + wgt_ref[2 * g:2 * g + 1, :] * jnp.maximum(zz[:span], 0.0)
            score = score + wgt_ref[2 * g + 1:2 * g + 2, :] * jnp.maximum(zz[span:], 0.0)
        bits = lax.bitcast_convert_type(score, I32)
        bits = jnp.where(bits == INT_MIN, 0, bits)
        key = bits ^ ((bits >> 31) & np.int32(0x7FFFFFFF))
        key_ref[rows(c), :] = jnp.where(span_causal(c), key, INT_MIN)
        return carry

    lax.fori_loop(0, nspan, score_span, 0)

    def count_ge(cand):
        def step(c, acc):
            return acc + fold(jnp.where(key_ref[rows(c), :] >= cand, 1, 0).astype(I32), jnp.sum)
        acc = lax.fori_loop(0, nspan, step, jnp.zeros((SUBLANES, tb), I32))
        return jnp.sum(acc, axis=0, keepdims=True)

    def refine(cand, state):
        thr, cnt = state
        c = count_ge(cand)
        ok = c >= n_sel
        return jnp.where(ok, cand, thr), jnp.where(ok, c, cnt)

    state = (jnp.full((1, tb), INT_MIN, I32), jnp.zeros((1, tb), I32) + nspan * span)
    state = refine(jnp.zeros((1, tb), I32), state)

    def bit_pass(it, state):
        return refine(state[0] + jnp.left_shift(np.int32(1), 30 - it), state)

    thr, cnt = lax.fori_loop(0, 31, bit_pass, state)

    def plain_bias(c, carry):
        sel = (key_ref[rows(c), :] >= thr) & span_causal(c)
        bias_ref[rows(c), :] = jnp.where(sel, 0.0, -jnp.inf)
        return carry

    lax.fori_loop(0, nspan, plain_bias, 0)

    tie = (cnt > n_sel) & (thr > INT_MIN)

    @pl.when(jnp.max(tie.astype(I32)) > 0)
    def _():
        def count_gt(j, acc):
            return acc + fold(jnp.where(key_ref[blk(j), :] > thr, 1, 0).astype(I32), jnp.sum)
        n_gt = jnp.sum(lax.fori_loop(0, nblk, count_gt, jnp.zeros((SUBLANES, tb), I32)),
                       axis=0, keepdims=True)
        need = jnp.where(tie, (n_sel - n_gt).astype(F32), 3.0e38)
        before = jnp.where(col < row, 1.0, 0.0).astype(BF16)

        def tie_bias(j, seen):
            kk = key_ref[blk(j), :]
            eq = kk == thr
            eqf = jnp.where(eq, 1.0, 0.0)
            rank = jnp.dot(before, eqf.astype(BF16), preferred_element_type=F32) + seen
            sel = ((kk > thr) | (eq & (rank < need))) & causal(j)
            bias_ref[blk(j), :] = jnp.where(sel, 0.0, -jnp.inf)
            return seen + jnp.sum(eqf, axis=0, keepdims=True)

        lax.fori_loop(0, nblk, tie_bias, jnp.zeros((1, tb), F32))

    def logits_span(c, mx):
        lhs = pair_lhs(kd_ref, c)
        bias = bias_ref[rows(c), :]
        new = []
        for g in range(N_HEADS // 2):
            zz = lax.dot_general(lhs, qd_ref[0, :, g * LANES:(g + 1) * LANES], _NT,
                                 preferred_element_type=F32)
            for hh in range(2):
                lg = zz[hh * span:(hh + 1) * span] + bias
                lg_ref[2 * g + hh, rows(c), :] = lg
                new.append(jnp.maximum(mx[2 * g + hh], fold(lg, jnp.max)))
        return tuple(new)

    ninf = jnp.full((SUBLANES, tb), -jnp.inf, F32)
    mx = lax.fori_loop(0, nspan, logits_span, (ninf,) * N_HEADS)
    mx = tuple(jnp.max(m, axis=0, keepdims=True) for m in mx)

    def pv_span(c, carry):
        vt = jnp.concatenate([vt_ref[0, c * SEARCH_BLOCKS + k] for k in range(SEARCH_BLOCKS)],
                             axis=1)
        new = []
        for h in range(N_HEADS):
            den, out = carry[h]
            p = jnp.exp(lg_ref[h, rows(c), :] - mx[h])
            out = out + jnp.dot(vt, p.astype(BF16), preferred_element_type=F32)
            new.append((den + fold(p, jnp.sum), out))
        return tuple(new)

    zero = (jnp.zeros((SUBLANES, tb), F32), jnp.zeros((HEAD_DIM, tb), F32))
    res = lax.fori_loop(0, nspan, pv_span, (zero,) * N_HEADS)
    for h in range(N_HEADS):
        den, out = res[h]
        out = out / jnp.sum(den, axis=0, keepdims=True)
        o_ref[0, h * HEAD_DIM:(h + 1) * HEAD_DIM, :] = out.astype(BF16)


def _dsa_attention(qd, qi, kd, ki, vt, wgt, b, s):
    tb = ATT_BLOCK
    nq = s // tb
    assert nq % SEARCH_BLOCKS == 0, s
    n_sel = min(TOPK_MAX, s // 4)
    return pl.pallas_call(
        functools.partial(_dsa_kernel, tb=tb, n_sel=n_sel),
        grid=(b, nq),
        in_specs=[pl.BlockSpec((1, tb, W_BRANCH), lambda bi, i: (bi, i, 0)),
                  pl.BlockSpec((1, tb, W_BRANCH), lambda bi, i: (bi, i, 0)),
                  pl.BlockSpec((1, s, 2 * LANES), lambda bi, i: (bi, 0, 0)),
                  pl.BlockSpec((1, s, 2 * LANES), lambda bi, i: (bi, 0, 0)),
                  pl.BlockSpec((1, nq, HEAD_DIM, tb), lambda bi, i: (bi, 0, 0, 0)),
                  pl.BlockSpec((N_HEADS, tb), lambda bi, i: (0, bi * nq + i))],
        out_specs=pl.BlockSpec((1, W_BRANCH, tb), lambda bi, i: (bi, 0, i)),
        out_shape=jax.ShapeDtypeStruct((b, W_BRANCH, s), BF16),
        scratch_shapes=[pltpu.VMEM((s, tb), I32), pltpu.VMEM((s, tb), F32),
                        pltpu.VMEM((N_HEADS, s, tb), F32)],
        compiler_params=_params(2),
        name="dsa_attention",
    )(qi, qd, ki, kd, vt, wgt)


def _merge_kernel(h_ref, g_ref, ysb_ref, yds_ref, wg_ref, wosb_ref, wods_ref, wo_ref, o_ref):
    h = h_ref[...]
    d = h.shape[1]
    u = _rmsnorm(h, g_ref[...]).astype(BF16)
    gates = jax.nn.sigmoid(jnp.dot(u, wg_ref[...], preferred_element_type=F32))
    y_sb = jnp.dot(ysb_ref[...], wosb_ref[...], preferred_element_type=F32)
    y_ds = jnp.dot(yds_ref[...], wods_ref[...], preferred_element_type=F32)
    merged = gates[:, :d] * y_sb + gates[:, d:] * y_ds
    o_ref[...] = h + jnp.dot(merged.astype(BF16), wo_ref[...], preferred_element_type=F32)


def _merge_out(hf, norm, y_sb, y_dsa, wgate, w_out_sb, w_out_dsa, w_out):
    n, d = hf.shape
    tm = _token_tile(n)
    wb = W_BRANCH

    def tok(w):
        return pl.BlockSpec((tm, w), lambda i: (i, 0))

    return pl.pallas_call(
        _merge_kernel,
        grid=(n // tm,),
        in_specs=[tok(d), _full((1, d)), tok(wb), tok(wb), _full(wgate.shape),
                  _full((wb, d)), _full((wb, d)), _full((d, d))],
        out_specs=tok(d),
        out_shape=jax.ShapeDtypeStruct((n, d), F32),
        compiler_params=_params(1),
        name="merge_out",
    )(hf, norm.reshape(1, d), y_sb, y_dsa, wgate,
      w_out_sb.astype(BF16), w_out_dsa.astype(BF16), w_out.astype(BF16))


def _ple_kernel(h_ref, g_ref, p_ref, wg_ref, wp_ref, gf_ref, o_ref, *, final):
    h = h_ref[...]
    u = _rmsnorm(h, g_ref[...]).astype(BF16)
    gate = jax.nn.sigmoid(jnp.dot(u, wg_ref[...], preferred_element_type=F32))
    emb = jnp.dot(p_ref[...].astype(BF16), wp_ref[...], preferred_element_type=F32)
    h = h + gate * emb
    o_ref[...] = _rmsnorm(h, gf_ref[...]) if final else h


def _ple_final(hf, norm, pf, w_gate, w_proj, final_norm, final):
    n, d = hf.shape
    tm = _token_tile(n)
    dp = pf.shape[1]

    def tok(w):
        return pl.BlockSpec((tm, w), lambda i: (i, 0))

    return pl.pallas_call(
        functools.partial(_ple_kernel, final=final),
        grid=(n // tm,),
        in_specs=[tok(d), _full((1, d)), tok(dp), _full((d, d)), _full((dp, d)), _full((1, d))],
        out_specs=tok(d),
        out_shape=jax.ShapeDtypeStruct((n, d), F32),
        compiler_params=_params(1),
        name="ple_final",
    )(hf, norm.reshape(1, d), pf, w_gate.astype(BF16), w_proj.astype(BF16),
      final_norm.reshape(1, d))


def kernel(x, p, positions, ffn1_norm, ffn1_w1, ffn1_w2, mix_norm, w_in, w_out_sb, w_out_dsa,
           w_out, ffn2_norm, ffn2_w1, ffn2_w2, ple_norm, ple_w_gate, ple_w_proj, final_norm):
    b, s, d = x.shape
    n = b * s
    depth = p.shape[0]
    nq = s // ATT_BLOCK
    h = x.reshape(n, d)
    for i in range(depth):
        h = _ffn_half_step(h, ffn1_norm[i], ffn1_w1[i], ffn1_w2[i])
        (sb, qd, qi, kd, ki, vt, wgt), wgate = _mixer_proj(h, positions, mix_norm[i], w_in[i])
        y_sb = _sb_attention(sb.reshape(b, s, 3 * W_BRANCH), b, s)
        vt4 = vt.reshape(HEAD_DIM, b, nq, ATT_BLOCK).transpose(1, 2, 0, 3)
        y_dsa_t = _dsa_attention(qd.reshape(b, s, W_BRANCH), qi.reshape(b, s, W_BRANCH),
                                 kd.reshape(b, s, 2 * LANES), ki.reshape(b, s, 2 * LANES),
                                 vt4, wgt, b, s)
        y_dsa = y_dsa_t.transpose(0, 2, 1).reshape(n, W_BRANCH)
        h = _merge_out(h, mix_norm[i], y_sb.reshape(n, W_BRANCH), y_dsa, wgate,
                       w_out_sb[i], w_out_dsa[i], w_out[i])
        h = _ffn_half_step(h, ffn2_norm[i], ffn2_w1[i], ffn2_w2[i])
        h = _ple_final(h, ple_norm[i], p[i].reshape(n, -1), ple_w_gate[i], ple_w_proj[i],
                       final_norm, final=(i == depth - 1))
    return h.reshape(b, s, d)
```

```python
import functools

import numpy as np
import jax
import jax.numpy as jnp
from jax import lax
from jax.experimental import pallas as pl
from jax.experimental.pallas import tpu as pltpu

F32 = jnp.float32
BF16 = jnp.bfloat16
I32 = jnp.int32
I16 = jnp.int16

HEAD_DIM = 64
N_HEADS = 8
W_BRANCH = N_HEADS * HEAD_DIM
TOPK_MAX = 256
ROPE_THETA = 500000.0
ROPE_DIM = HEAD_DIM // 4
EPS = 1e-6
LOG2E = 1.4426950408889634

LANES = 128
SUBLANES = 8
ATT_BLOCK = 128
DSA_QUERIES = 256
DSA_SPAN = 512
PACKED_SUBLANES = 16
INT_MIN = np.int32(-2**31)
INT16_MIN = -2**15
VMEM_LIMIT = 56 * 1024 * 1024

_NT = (((1,), (1,)), ((), ()))


def _rmsnorm(x, g):
    ms = jnp.mean(x * x, axis=-1, keepdims=True)
    return x * lax.rsqrt(ms + EPS) * g


def _token_tile(n):
    for tm in (512, 256, 128):
        if n % tm == 0:
            return tm
    raise ValueError(f"token count {n} must be a multiple of 128")


def _ff_chunks(d_ff):
    assert d_ff % 256 == 0, d_ff
    chunks, left = [], d_ff
    while left:
        c = min(768, left)
        chunks.append(c)
        left -= c
    return tuple(chunks)


def _params(n_axes):
    return pltpu.CompilerParams(dimension_semantics=("arbitrary",) * n_axes,
                                vmem_limit_bytes=VMEM_LIMIT)


def _full(shape):
    nd = len(shape)
    return pl.BlockSpec(shape, lambda *_: (0,) * nd)


def _ffn_kernel(x_ref, g_ref, w1a_ref, w1b_ref, w2_ref, o_ref, *, chunks):
    x = x_ref[...]
    xn = _rmsnorm(x, g_ref[...]).astype(BF16)
    acc = jnp.zeros(x.shape, F32)
    off = 0
    for cw in chunks:
        a = jnp.dot(xn, w1a_ref[:, off:off + cw], preferred_element_type=F32)
        b = jnp.dot(xn, w1b_ref[:, off:off + cw], preferred_element_type=F32)
        hid = (a * jax.nn.sigmoid(a) * b).astype(BF16)
        acc = acc + jnp.dot(hid, w2_ref[off:off + cw, :], preferred_element_type=F32)
        off += cw
    o_ref[...] = x + 0.5 * acc


def _ffn_half_step(xf, norm, w1, w2):
    n, d = xf.shape
    d_ff = w2.shape[0]
    tm = _token_tile(n)
    w1a = w1[:, :d_ff].astype(BF16)
    w1b = w1[:, d_ff:].astype(BF16)
    w2b = w2.astype(BF16)
    tok = pl.BlockSpec((tm, d), lambda i: (i, 0))
    return pl.pallas_call(
        functools.partial(_ffn_kernel, chunks=_ff_chunks(d_ff)),
        grid=(n // tm,),
        in_specs=[tok, _full((1, d)), _full(w1a.shape), _full(w1b.shape), _full(w2b.shape)],
        out_specs=tok,
        out_shape=jax.ShapeDtypeStruct((n, d), F32),
        compiler_params=_params(1),
        name="ffn_half_step",
    )(xf, norm.reshape(1, d), w1a, w1b, w2b)


def _proj_kernel(h_ref, g_ref, pos_ref, invf_ref, wsb_ref, wrot_ref, wt_ref,
                 sb_ref, qd_ref, qi_ref, kd_ref, ki_ref, vt_ref, wgt_ref):
    u = _rmsnorm(h_ref[...], g_ref[...]).astype(BF16)
    sb_ref[...] = jnp.dot(u, wsb_ref[...], preferred_element_type=F32).astype(BF16)

    ang = pos_ref[...].astype(F32) * invf_ref[...]
    cos = jnp.cos(ang)
    sin = jnp.sin(ang)
    lane = lax.broadcasted_iota(I32, (1, LANES), 1) % HEAD_DIM
    half = ROPE_DIM // 2
    s_lo = jnp.where(lane < half, -sin, 0.0)
    s_hi = jnp.where((lane >= half) & (lane < ROPE_DIM), sin, 0.0)
    rot = jnp.dot(u, wrot_ref[...], preferred_element_type=F32)
    outs = ((qd_ref, 0, W_BRANCH), (qi_ref, W_BRANCH, W_BRANCH),
            (kd_ref, 2 * W_BRANCH, 2 * LANES), (ki_ref, 2 * W_BRANCH + 2 * LANES, 2 * LANES))
    for ref, base, width in outs:
        for g in range(width // LANES):
            xg = rot[:, base + g * LANES: base + (g + 1) * LANES]
            yg = (xg * cos + pltpu.roll(xg, LANES - half, 1) * s_lo
                  + pltpu.roll(xg, half, 1) * s_hi)
            ref[:, g * LANES:(g + 1) * LANES] = yg.astype(BF16)

    tr = lax.dot_general(wt_ref[...], u, _NT, preferred_element_type=F32)
    vt_ref[...] = tr[:HEAD_DIM].astype(BF16)
    wgt_ref[...] = tr[HEAD_DIM:] * (N_HEADS ** -0.5)


def _mixer_proj(hf, positions, norm, w_in):
    n, d = hf.shape
    tm = _token_tile(n)
    wb = W_BRANCH
    o = 0
    cols = {}
    for name, size in (("q_sb", wb), ("k_sb", wb), ("v_sb", wb), ("q_d", wb), ("k_d", HEAD_DIM),
                       ("v_d", HEAD_DIM), ("q_i", wb), ("k_i", HEAD_DIM), ("w_i", N_HEADS),
                       ("g_sb", d), ("g_dsa", d)):
        cols[name] = w_in[:, o:o + size]
        o += size
    scale = HEAD_DIM ** -0.5
    zeros = jnp.zeros((d, HEAD_DIM), w_in.dtype)
    wsb = jnp.concatenate([cols["q_sb"] * (scale * LOG2E), cols["k_sb"], cols["v_sb"]],
                          axis=1).astype(BF16)
    wrot = jnp.concatenate([cols["q_d"] * (scale * LOG2E), cols["q_i"] * scale,
                            cols["k_d"], zeros, zeros, cols["k_d"],
                            cols["k_i"], zeros, zeros, cols["k_i"]], axis=1).astype(BF16)
    wt = jnp.concatenate([cols["v_d"], cols["w_i"]], axis=1).T.astype(BF16)
    wgate = jnp.concatenate([cols["g_sb"], cols["g_dsa"]], axis=1).astype(BF16)

    lane = np.arange(LANES) % HEAD_DIM
    inv_freq = ROPE_THETA ** (-jnp.arange(0, ROPE_DIM, 2, dtype=F32) / ROPE_DIM)
    invf = jnp.where(lane < ROPE_DIM, inv_freq[lane % (ROPE_DIM // 2)], 0.0).reshape(1, LANES)

    def tok(w):
        return pl.BlockSpec((tm, w), lambda i: (i, 0))

    def tok_t(r):
        return pl.BlockSpec((r, tm), lambda i: (0, i))

    outs = pl.pallas_call(
        _proj_kernel,
        grid=(n // tm,),
        in_specs=[tok(d), _full((1, d)), tok(1), _full((1, LANES)),
                  _full(wsb.shape), _full(wrot.shape), _full(wt.shape)],
        out_specs=[tok(3 * wb), tok(wb), tok(wb), tok(2 * LANES), tok(2 * LANES),
                   tok_t(HEAD_DIM), tok_t(N_HEADS)],
        out_shape=[jax.ShapeDtypeStruct((n, 3 * wb), BF16),
                   jax.ShapeDtypeStruct((n, wb), BF16),
                   jax.ShapeDtypeStruct((n, wb), BF16),
                   jax.ShapeDtypeStruct((n, 2 * LANES), BF16),
                   jax.ShapeDtypeStruct((n, 2 * LANES), BF16),
                   jax.ShapeDtypeStruct((HEAD_DIM, n), BF16),
                   jax.ShapeDtypeStruct((N_HEADS, n), F32)],
        compiler_params=_params(1),
        name="mixer_proj",
    )(hf, norm.reshape(1, d), positions.reshape(n, 1), invf, wsb, wrot, wt)
    return outs, wgate


def _sb_kernel(q_ref, k_ref, v_ref, o_ref, qm_ref, acc_ref, car_ref, *, tq, sub):
    i = pl.program_id(1)
    nsub = tq // sub
    pairs = W_BRANCH // LANES
    lane = lax.broadcasted_iota(I32, (tq, LANES), 1)
    lo_half = lane < HEAD_DIM
    q = q_ref[0]
    for g in range(pairs):
        qg = q[:, g * LANES:(g + 1) * LANES]
        qm_ref[2 * g] = jnp.where(lo_half, qg, jnp.zeros_like(qg))
        qm_ref[2 * g + 1] = jnp.where(lo_half, jnp.zeros_like(qg), qg)
    r = lax.broadcasted_iota(I32, (2 * sub, 2 * sub), 0) % sub
    c = lax.broadcasted_iota(I32, (2 * sub, 2 * sub), 1)
    suffix = jnp.where((c >= sub) | (r > c), 1.0, 0.0).astype(BF16)
    strict = (lax.broadcasted_iota(I32, (tq, tq), 1) < lax.broadcasted_iota(I32, (tq, tq), 0))

    acc_ref[...] = jnp.zeros(acc_ref.shape, F32)
    car_ref[...] = jnp.zeros(car_ref.shape, F32)

    def group(j, diag):
        rows = pl.ds(pl.multiple_of(j * tq, tq), tq)
        hl, zl = [], []
        for g in range(pairs):
            kg = k_ref[0, rows, g * LANES:(g + 1) * LANES]
            zz = lax.dot_general(jnp.concatenate([qm_ref[2 * g], qm_ref[2 * g + 1]], axis=0), kg,
                                 _NT, preferred_element_type=F32)
            for hh in range(2):
                z = zz[hh * tq:(hh + 1) * tq]
                neg_abs = lax.bitcast_convert_type(lax.bitcast_convert_type(z, I32) | INT_MIN, F32)
                sp = jnp.maximum(z, 0.0) + jnp.log(1.0 + jnp.exp2(neg_abs)) * LOG2E
                if diag:
                    sp = jnp.where(strict, sp, 0.0)
                for sb in range(nsub):
                    part = sp[:, sb * sub:(sb + 1) * sub]
                    hi = part.astype(BF16)
                    lo = (part - hi.astype(F32)).astype(BF16)
                    hl.append(jnp.concatenate([hi, lo], axis=1))
                zl.append(z - sp)
        sums = jnp.dot(jnp.concatenate(hl, axis=0), suffix, preferred_element_type=F32)
        for g in range(pairs):
            probs = []
            for hh in range(2):
                h = 2 * g + hh
                run = car_ref[h]
                parts = [None] * nsub
                for sb in reversed(range(nsub)):
                    blk = sums[(h * nsub + sb) * tq:(h * nsub + sb + 1) * tq]
                    a = jnp.exp2(zl[h][:, sb * sub:(sb + 1) * sub] - (blk[:, :sub] + run))
                    if diag:
                        a = jnp.where(strict[:, sb * sub:(sb + 1) * sub], a, 0.0)
                    parts[sb] = a.astype(BF16)
                    run = run + blk[:, sub:]
                car_ref[h] = run
                probs.extend(parts)
            vg = v_ref[0, rows, g * LANES:(g + 1) * LANES]
            vv = jnp.concatenate([jnp.where(lo_half, vg, jnp.zeros_like(vg)),
                                  jnp.where(lo_half, jnp.zeros_like(vg), vg)], axis=0)
            acc_ref[g] += jnp.dot(jnp.concatenate(probs, axis=1), vv, preferred_element_type=F32)

    group(i, True)

    def body(jj, carry):
        group(i - 1 - jj, False)
        return carry

    lax.fori_loop(0, i, body, 0)
    for g in range(pairs):
        o_ref[0, :, g * LANES:(g + 1) * LANES] = acc_ref[g].astype(BF16)


def _sb_attention(qkv, b, s):
    sub = ATT_BLOCK
    tq = 2 * sub if s % (2 * sub) == 0 else sub
    wb = W_BRANCH
    return pl.pallas_call(
        functools.partial(_sb_kernel, tq=tq, sub=sub),
        grid=(b, s // tq),
        in_specs=[pl.BlockSpec((1, tq, wb), lambda bi, i: (bi, i, 0)),
                  pl.BlockSpec((1, s, wb), lambda bi, i: (bi, 0, 1)),
                  pl.BlockSpec((1, s, wb), lambda bi, i: (bi, 0, 2))],
        out_specs=pl.BlockSpec((1, tq, wb), lambda bi, i: (bi, i, 0)),
        out_shape=jax.ShapeDtypeStruct((b, s, wb), BF16),
        scratch_shapes=[pltpu.VMEM((N_HEADS, tq, LANES), BF16),
                        pltpu.VMEM((wb // LANES, tq, LANES), F32),
                        pltpu.VMEM((N_HEADS, tq, LANES), F32)],
        compiler_params=_params(2),
        name="sb_attention",
    )(qkv, qkv, qkv)


def _dsa_kernel(qi_ref, qd_ref, ki_ref, kd_ref, vt_ref, wgt_ref, o_ref,
                key_ref, hi_ref, lo_ref, lom_ref, bias_ref, lg_ref, *, tb, span, n_sel):
    i = pl.program_id(1)
    nspan = ((i + 1) * tb - 1) // span + 1
    srow = lax.broadcasted_iota(I32, (span, tb), 0)
    scol = lax.broadcasted_iota(I32, (span, tb), 1)

    def rows(c):
        return pl.ds(pl.multiple_of(c * span, span), span)

    def span_causal(c):
        return (c * span + srow) <= (i * tb + scol)

    def fold(x, op):
        return op(x.reshape(x.shape[0] // SUBLANES, SUBLANES, x.shape[1]), axis=0)

    def pair_lhs(ref, c):
        kab = ref[0, rows(c), :]
        return jnp.concatenate([kab[:, :LANES], kab[:, LANES:]], axis=0)

    def head_dots(lhs, q_ref):
        out = []
        for gg in range(N_HEADS // 4):
            qq = jnp.concatenate([q_ref[0, :, (2 * gg) * LANES:(2 * gg + 1) * LANES],
                                  q_ref[0, :, (2 * gg + 1) * LANES:(2 * gg + 2) * LANES]], axis=0)
            zz = lax.dot_general(lhs, qq, _NT, preferred_element_type=F32)
            out += [zz[:span, :tb], zz[span:, :tb], zz[:span, tb:], zz[span:, tb:]]
        return out

    def score_span(c, carry):
        lhs = pair_lhs(ki_ref, c)
        score = jnp.zeros((span, tb), F32)
        for h, dots in enumerate(head_dots(lhs, qi_ref)):
            score = score + wgt_ref[h:h + 1, :] * jnp.maximum(dots, 0.0)
        bits = lax.bitcast_convert_type(score, I32)
        bits = jnp.where(bits == INT_MIN, 0, bits)
        key = bits ^ ((bits >> 31) & np.int32(0x7FFFFFFF))
        key = jnp.where(span_causal(c), key, INT_MIN)
        key_ref[rows(c), :] = key
        hi_ref[rows(c), :] = (key >> 16).astype(I16)
        lo_ref[rows(c), :] = ((key & np.int32(0xFFFF)) + INT16_MIN).astype(I16)
        return carry

    lax.fori_loop(0, nspan, score_span, 0)

    def count16(ref, pred):
        n_acc = 8

        def step(c, accs):
            kk = ref[rows(c), :].reshape(span // PACKED_SUBLANES, PACKED_SUBLANES, tb)
            accs = list(accs)
            for r in range(span // PACKED_SUBLANES):
                a = accs[r % n_acc]
                accs[r % n_acc] = jnp.where(pred(kk[r]), a + np.int16(1), a)
            return tuple(accs)

        accs = lax.fori_loop(0, nspan, step, (jnp.zeros((PACKED_SUBLANES, tb), I16),) * n_acc)
        return jnp.sum(sum(accs[1:], accs[0]).astype(I32), axis=0, keepdims=True)

    def packed(x):
        return jnp.broadcast_to(x, (PACKED_SUBLANES, tb)).astype(I16)

    def search16(ref, base, state):
        def refine(cand, state):
            thr, cnt = state
            cand16 = packed(cand)
            c = base + count16(ref, lambda kk: kk >= cand16)
            ok = c >= n_sel
            return jnp.where(ok, cand, thr), jnp.where(ok, c, cnt)

        state = refine(jnp.zeros((1, tb), I32), state)
        return lax.fori_loop(
            0, 15, lambda it, st: refine(st[0] + jnp.left_shift(np.int32(1), 14 - it), st), state)

    lowest = jnp.full((1, tb), INT16_MIN, I32)
    thr_hi, cnt = search16(hi_ref, 0, (lowest, jnp.zeros((1, tb), I32) + nspan * span))
    thr_hi16 = packed(thr_hi)

    def match_span(c, carry):
        hi = hi_ref[rows(c), :].reshape(span // PACKED_SUBLANES, PACKED_SUBLANES, tb)
        lo = lo_ref[rows(c), :].reshape(span // PACKED_SUBLANES, PACKED_SUBLANES, tb)
        lom_ref[rows(c), :] = jnp.where(hi == thr_hi16[None], lo, np.int16(INT16_MIN)
                                        ).reshape(span, tb)
        return carry

    lax.fori_loop(0, nspan, match_span, 0)
    n_above = count16(hi_ref, lambda kk: kk > thr_hi16)
    thr_lo, cnt = search16(lom_ref, n_above, (lowest, cnt))
    thr = thr_hi * 65536 + (thr_lo - INT16_MIN)

    def plain_bias(c, carry):
        sel = (key_ref[rows(c), :] >= thr) & span_causal(c)
        bias_ref[rows(c), :] = jnp.where(sel, 0.0, -jnp.inf)
        return carry

    lax.fori_loop(0, nspan, plain_bias, 0)

    tie = (cnt > n_sel) & (thr > INT_MIN)

    @pl.when(jnp.max(tie.astype(I32)) > 0)
    def _():
        def count_gt(c, acc):
            return acc + fold(jnp.where(key_ref[rows(c), :] > thr, 1, 0).astype(I32), jnp.sum)
        n_gt = jnp.sum(lax.fori_loop(0, nspan, count_gt, jnp.zeros((SUBLANES, tb), I32)),
                       axis=0, keepdims=True)
        need = jnp.where(tie, (n_sel - n_gt).astype(F32), 3.0e38)
        before = jnp.where(lax.broadcasted_iota(I32, (span, span), 1)
                           < lax.broadcasted_iota(I32, (span, span), 0),
                           1.0, 0.0).astype(BF16)

        def tie_bias(c, seen):
            kk = key_ref[rows(c), :]
            eq = kk == thr
            eqf = jnp.where(eq, 1.0, 0.0)
            rank = jnp.dot(before, eqf.astype(BF16), preferred_element_type=F32) + seen
            sel = ((kk > thr) | (eq & (rank < need))) & span_causal(c)
            bias_ref[rows(c), :] = jnp.where(sel, 0.0, -jnp.inf)
            return seen + jnp.sum(eqf, axis=0, keepdims=True)

        lax.fori_loop(0, nspan, tie_bias, jnp.zeros((1, tb), F32))

    def logits_span(c, mx):
        lhs = pair_lhs(kd_ref, c)
        bias = bias_ref[rows(c), :]
        new = []
        for h, dots in enumerate(head_dots(lhs, qd_ref)):
            lg = dots + bias
            lg_ref[h, rows(c), :] = lg
            new.append(jnp.maximum(mx[h], fold(lg, jnp.max)))
        return tuple(new)

    ninf = jnp.full((SUBLANES, tb), -jnp.inf, F32)
    mx = lax.fori_loop(0, nspan, logits_span, (ninf,) * N_HEADS)
    mx = tuple(jnp.max(m, axis=0, keepdims=True) for m in mx)

    def pv_span(c, outs):
        vt = jnp.concatenate([vt_ref[0, c * (span // LANES) + k] for k in range(span // LANES)],
                             axis=1)
        vt1 = jnp.concatenate([vt, jnp.ones((PACKED_SUBLANES, span), BF16)], axis=0)
        new = []
        for h in range(N_HEADS):
            p = jnp.exp2(lg_ref[h, rows(c), :] - mx[h])
            new.append(outs[h] + jnp.dot(vt1, p.astype(BF16), preferred_element_type=F32))
        return tuple(new)

    zero = jnp.zeros((HEAD_DIM + PACKED_SUBLANES, tb), F32)
    res = lax.fori_loop(0, nspan, pv_span, (zero,) * N_HEADS)
    for h in range(N_HEADS):
        out = res[h][:HEAD_DIM] / res[h][HEAD_DIM:HEAD_DIM + 1]
        o_ref[0, h * HEAD_DIM:(h + 1) * HEAD_DIM, :] = out.astype(BF16)


def _dsa_attention(qd, qi, kd, ki, vt, wgt, b, s):
    tb = DSA_QUERIES
    span = DSA_SPAN
    assert s % span == 0 and span % tb == 0, s
    nq = s // tb
    n_sel = min(TOPK_MAX, s // 4)
    return pl.pallas_call(
        functools.partial(_dsa_kernel, tb=tb, span=span, n_sel=n_sel),
        grid=(b, nq),
        in_specs=[pl.BlockSpec((1, tb, W_BRANCH), lambda bi, i: (bi, i, 0)),
                  pl.BlockSpec((1, tb, W_BRANCH), lambda bi, i: (bi, i, 0)),
                  pl.BlockSpec((1, s, 2 * LANES), lambda bi, i: (bi, 0, 0)),
                  pl.BlockSpec((1, s, 2 * LANES), lambda bi, i: (bi, 0, 0)),
                  pl.BlockSpec((1, s // LANES, HEAD_DIM, LANES), lambda bi, i: (bi, 0, 0, 0)),
                  pl.BlockSpec((N_HEADS, tb), lambda bi, i: (0, bi * nq + i))],
        out_specs=pl.BlockSpec((1, W_BRANCH, tb), lambda bi, i: (bi, 0, i)),
        out_shape=jax.ShapeDtypeStruct((b, W_BRANCH, s), BF16),
        scratch_shapes=[pltpu.VMEM((s, tb), I32), pltpu.VMEM((s, tb), I16),
                        pltpu.VMEM((s, tb), I16), pltpu.VMEM((s, tb), I16),
                        pltpu.VMEM((s, tb), F32),
                        pltpu.VMEM((N_HEADS, s, tb), F32)],
        compiler_params=_params(2),
        name="dsa_attention",
    )(qi, qd, ki, kd, vt, wgt)


def _merge_kernel(h_ref, g_ref, ysb_ref, yds_ref, wg_ref, wosb_ref, wods_ref, wo_ref, o_ref):
    h = h_ref[...]
    d = h.shape[1]
    u = _rmsnorm(h, g_ref[...]).astype(BF16)
    gates = jax.nn.sigmoid(jnp.dot(u, wg_ref[...], preferred_element_type=F32))
    y_sb = jnp.dot(ysb_ref[...], wosb_ref[...], preferred_element_type=F32)
    y_ds = jnp.dot(yds_ref[...], wods_ref[...], preferred_element_type=F32)
    merged = gates[:, :d] * y_sb + gates[:, d:] * y_ds
    o_ref[...] = h + jnp.dot(merged.astype(BF16), wo_ref[...], preferred_element_type=F32)


def _merge_out(hf, norm, y_sb, y_dsa, wgate, w_out_sb, w_out_dsa, w_out):
    n, d = hf.shape
    tm = _token_tile(n)
    wb = W_BRANCH

    def tok(w):
        return pl.BlockSpec((tm, w), lambda i: (i, 0))

    return pl.pallas_call(
        _merge_kernel,
        grid=(n // tm,),
        in_specs=[tok(d), _full((1, d)), tok(wb), tok(wb), _full(wgate.shape),
                  _full((wb, d)), _full((wb, d)), _full((d, d))],
        out_specs=tok(d),
        out_shape=jax.ShapeDtypeStruct((n, d), F32),
        compiler_params=_params(1),
        name="merge_out",
    )(hf, norm.reshape(1, d), y_sb, y_dsa, wgate,
      w_out_sb.astype(BF16), w_out_dsa.astype(BF16), w_out.astype(BF16))


def _ple_kernel(h_ref, g_ref, p_ref, wg_ref, wp_ref, gf_ref, o_ref, *, final):
    h = h_ref[...]
    u = _rmsnorm(h, g_ref[...]).astype(BF16)
    gate = jax.nn.sigmoid(jnp.dot(u, wg_ref[...], preferred_element_type=F32))
    emb = jnp.dot(p_ref[...].astype(BF16), wp_ref[...], preferred_element_type=F32)
    h = h + gate * emb
    o_ref[...] = _rmsnorm(h, gf_ref[...]) if final else h


def _ple_final(hf, norm, pf, w_gate, w_proj, final_norm, final):
    n, d = hf.shape
    tm = _token_tile(n)
    dp = pf.shape[1]

    def tok(w):
        return pl.BlockSpec((tm, w), lambda i: (i, 0))

    return pl.pallas_call(
        functools.partial(_ple_kernel, final=final),
        grid=(n // tm,),
        in_specs=[tok(d), _full((1, d)), tok(dp), _full((d, d)), _full((dp, d)), _full((1, d))],
        out_specs=tok(d),
        out_shape=jax.ShapeDtypeStruct((n, d), F32),
        compiler_params=_params(1),
        name="ple_final",
    )(hf, norm.reshape(1, d), pf, w_gate.astype(BF16), w_proj.astype(BF16),
      final_norm.reshape(1, d))


def kernel(x, p, positions, ffn1_norm, ffn1_w1, ffn1_w2, mix_norm, w_in, w_out_sb, w_out_dsa,
           w_out, ffn2_norm, ffn2_w1, ffn2_w2, ple_norm, ple_w_gate, ple_w_proj, final_norm):
    b, s, d = x.shape
    n = b * s
    depth = p.shape[0]
    nq = s // ATT_BLOCK
    h = x.reshape(n, d)
    for i in range(depth):
        h = _ffn_half_step(h, ffn1_norm[i], ffn1_w1[i], ffn1_w2[i])
        (sb, qd, qi, kd, ki, vt, wgt), wgate = _mixer_proj(h, positions, mix_norm[i], w_in[i])
        y_sb = _sb_attention(sb.reshape(b, s, 3 * W_BRANCH), b, s)
        vt4 = vt.reshape(HEAD_DIM, b, nq, ATT_BLOCK).transpose(1, 2, 0, 3)
        y_dsa_t = _dsa_attention(qd.reshape(b, s, W_BRANCH), qi.reshape(b, s, W_BRANCH),
                                 kd.reshape(b, s, 2 * LANES), ki.reshape(b, s, 2 * LANES),
                                 vt4, wgt, b, s)
        y_dsa = y_dsa_t.transpose(0, 2, 1).reshape(n, W_BRANCH)
        h = _merge_out(h, mix_norm[i], y_sb.reshape(n, W_BRANCH), y_dsa, wgate,
                       w_out_sb[i], w_out_dsa[i], w_out[i])
        h = _ffn_half_step(h, ffn2_norm[i], ffn2_w1[i], ffn2_w2[i])
        h = _ple_final(h, ple_norm[i], p[i].reshape(n, -1), ple_w_gate[i], ple_w_proj[i],
                       final_norm, final=(i == depth - 1))
    return h.reshape(b, s, d)
```

```python
import functools

import numpy as np
import jax
import jax.numpy as jnp
from jax import lax
from jax.experimental import pallas as pl
from jax.experimental.pallas import tpu as pltpu

F32 = jnp.float32
BF16 = jnp.bfloat16
I32 = jnp.int32
I16 = jnp.int16

HEAD_DIM = 64
N_HEADS = 8
W_BRANCH = N_HEADS * HEAD_DIM
TOPK_MAX = 256
ROPE_THETA = 500000.0
ROPE_DIM = HEAD_DIM // 4
EPS = 1e-6
LOG2E = 1.4426950408889634

LANES = 128
SUBLANES = 8
ATT_BLOCK = 128
DSA_QUERIES = 256
DSA_SPAN = 512
PACKED_SUBLANES = 16
INT_MIN = np.int32(-2**31)
INT16_MIN = -2**15
VMEM_LIMIT = 56 * 1024 * 1024

_NT = (((1,), (1,)), ((), ()))


def _rmsnorm(x, g):
    ms = jnp.mean(x * x, axis=-1, keepdims=True)
    return x * lax.rsqrt(ms + EPS) * g


def _token_tile(n):
    for tm in (512, 256, 128):
        if n % tm == 0:
            return tm
    raise ValueError(f"token count {n} must be a multiple of 128")


def _ff_chunks(d_ff):
    assert d_ff % 256 == 0, d_ff
    chunks, left = [], d_ff
    while left:
        c = min(768, left)
        chunks.append(c)
        left -= c
    return tuple(chunks)


def _params(n_axes):
    return pltpu.CompilerParams(dimension_semantics=("arbitrary",) * n_axes,
                                vmem_limit_bytes=VMEM_LIMIT)


def _full(shape):
    nd = len(shape)
    return pl.BlockSpec(shape, lambda *_: (0,) * nd)


def _swiglu_half_step(x, g_ref, w1a_ref, w1b_ref, w2_ref, chunks):
    xn = _rmsnorm(x, g_ref[...]).astype(BF16)
    acc = jnp.zeros(x.shape, F32)
    off = 0
    for cw in chunks:
        a = jnp.dot(xn, w1a_ref[:, off:off + cw], preferred_element_type=F32)
        b = jnp.dot(xn, w1b_ref[:, off:off + cw], preferred_element_type=F32)
        hid = (a * jax.nn.sigmoid(a) * b).astype(BF16)
        acc = acc + jnp.dot(hid, w2_ref[off:off + cw, :], preferred_element_type=F32)
        off += cw
    return x + 0.5 * acc


def _ffn_kernel(x_ref, g_ref, w1a_ref, w1b_ref, w2_ref, o_ref, *, chunks):
    o_ref[...] = _swiglu_half_step(x_ref[...], g_ref, w1a_ref, w1b_ref, w2_ref, chunks)


def _ffn_ple_kernel(x_ref, g_ref, w1a_ref, w1b_ref, w2_ref, gp_ref, p_ref, wg_ref, wp_ref,
                    gf_ref, o_ref, *, chunks, final):
    h = _swiglu_half_step(x_ref[...], g_ref, w1a_ref, w1b_ref, w2_ref, chunks)
    u = _rmsnorm(h, gp_ref[...]).astype(BF16)
    gate = jax.nn.sigmoid(jnp.dot(u, wg_ref[...], preferred_element_type=F32))
    emb = jnp.dot(p_ref[...].astype(BF16), wp_ref[...], preferred_element_type=F32)
    h = h + gate * emb
    o_ref[...] = _rmsnorm(h, gf_ref[...]) if final else h


def _ffn_weights(w1, w2):
    d_ff = w2.shape[0]
    return w1[:, :d_ff].astype(BF16), w1[:, d_ff:].astype(BF16), w2.astype(BF16)


def _ffn_half_step(xf, norm, w1, w2):
    n, d = xf.shape
    tm = _token_tile(n)
    w1a, w1b, w2b = _ffn_weights(w1, w2)
    tok = pl.BlockSpec((tm, d), lambda i: (i, 0))
    return pl.pallas_call(
        functools.partial(_ffn_kernel, chunks=_ff_chunks(w2.shape[0])),
        grid=(n // tm,),
        in_specs=[tok, _full((1, d)), _full(w1a.shape), _full(w1b.shape), _full(w2b.shape)],
        out_specs=tok,
        out_shape=jax.ShapeDtypeStruct((n, d), F32),
        compiler_params=_params(1),
        name="ffn_half_step",
    )(xf, norm.reshape(1, d), w1a, w1b, w2b)


def _ffn_ple_step(xf, norm, w1, w2, ple_norm, pf, ple_w_gate, ple_w_proj, final_norm, final):
    n, d = xf.shape
    dp = pf.shape[1]
    tm = _token_tile(n)
    w1a, w1b, w2b = _ffn_weights(w1, w2)

    def tok(w):
        return pl.BlockSpec((tm, w), lambda i: (i, 0))

    return pl.pallas_call(
        functools.partial(_ffn_ple_kernel, chunks=_ff_chunks(w2.shape[0]), final=final),
        grid=(n // tm,),
        in_specs=[tok(d), _full((1, d)), _full(w1a.shape), _full(w1b.shape), _full(w2b.shape),
                  _full((1, d)), tok(dp), _full((d, d)), _full((dp, d)), _full((1, d))],
        out_specs=tok(d),
        out_shape=jax.ShapeDtypeStruct((n, d), F32),
        compiler_params=_params(1),
        name="ffn_ple_step",
    )(xf, norm.reshape(1, d), w1a, w1b, w2b, ple_norm.reshape(1, d), pf,
      ple_w_gate.astype(BF16), ple_w_proj.astype(BF16), final_norm.reshape(1, d))


def _proj_kernel(h_ref, g_ref, pos_ref, invf_ref, spread_ref, base_ref, wsb_ref, wrot_ref, wt_ref,
                 sb_ref, qd_ref, qi_ref, kd_ref, ki_ref, vt_ref, wgt_ref):
    u = _rmsnorm(h_ref[...], g_ref[...]).astype(BF16)
    sb_ref[...] = jnp.dot(u, wsb_ref[...], preferred_element_type=F32).astype(BF16)

    ang = invf_ref[...] * pos_ref[...].astype(F32)
    cs = jnp.concatenate([jnp.cos(ang), jnp.sin(ang)], axis=0)
    cs_hi = cs.astype(BF16)
    cs_lo = (cs - cs_hi.astype(F32)).astype(BF16)
    pat = lax.dot_general(jnp.concatenate([cs_hi, cs_lo], axis=0), spread_ref[...],
                          (((0,), (0,)), ((), ())), preferred_element_type=F32)
    half = ROPE_DIM // 2
    cos = pat[:, :LANES] + base_ref[...]
    s_lo = pat[:, LANES:2 * LANES]
    s_hi = pat[:, 2 * LANES:]
    rot = jnp.dot(u, wrot_ref[...], preferred_element_type=F32)
    outs = ((qd_ref, 0, W_BRANCH), (qi_ref, W_BRANCH, W_BRANCH),
            (kd_ref, 2 * W_BRANCH, 2 * LANES), (ki_ref, 2 * W_BRANCH + 2 * LANES, 2 * LANES))
    for ref, base, width in outs:
        for g in range(width // LANES):
            xg = rot[:, base + g * LANES: base + (g + 1) * LANES]
            yg = (xg * cos + pltpu.roll(xg, LANES - half, 1) * s_lo
                  + pltpu.roll(xg, half, 1) * s_hi)
            ref[:, g * LANES:(g + 1) * LANES] = yg.astype(BF16)

    tr = lax.dot_general(wt_ref[...], u, _NT, preferred_element_type=F32)
    for kb in range(vt_ref.shape[0]):
        vt_ref[kb] = tr[:HEAD_DIM, kb * LANES:(kb + 1) * LANES].astype(BF16)
    wgt_ref[...] = tr[HEAD_DIM:] * (N_HEADS ** -0.5)


def _mixer_proj(hf, positions, norm, w_in):
    n, d = hf.shape
    tm = _token_tile(n)
    wb = W_BRANCH
    o = 0
    cols = {}
    for name, size in (("q_sb", wb), ("k_sb", wb), ("v_sb", wb), ("q_d", wb), ("k_d", HEAD_DIM),
                       ("v_d", HEAD_DIM), ("q_i", wb), ("k_i", HEAD_DIM), ("w_i", N_HEADS),
                       ("g_sb", d), ("g_dsa", d)):
        cols[name] = w_in[:, o:o + size]
        o += size
    scale = HEAD_DIM ** -0.5
    zeros = jnp.zeros((d, HEAD_DIM), w_in.dtype)
    wsb = jnp.concatenate([cols["q_sb"] * (scale * LOG2E), cols["k_sb"], cols["v_sb"]],
                          axis=1).astype(BF16)
    wrot = jnp.concatenate([cols["q_d"] * (scale * LOG2E), cols["q_i"] * scale,
                            cols["k_d"], zeros, zeros, cols["k_d"],
                            cols["k_i"], zeros, zeros, cols["k_i"]], axis=1).astype(BF16)
    wt = jnp.concatenate([cols["v_d"], cols["w_i"]], axis=1).T.astype(BF16)
    wgate = jnp.concatenate([cols["g_sb"], cols["g_dsa"]], axis=1).astype(BF16)

    half = ROPE_DIM // 2
    invf = (ROPE_THETA ** (-jnp.arange(0, ROPE_DIM, 2, dtype=F32) / ROPE_DIM)).reshape(half, 1)
    lane = np.arange(LANES) % HEAD_DIM
    hit = (lane[None, :] % half == np.arange(half)[:, None])
    zero = np.zeros((half, LANES), np.float32)
    cos_rows = np.concatenate([hit & (lane < ROPE_DIM), zero, zero], axis=1)
    sin_rows = np.concatenate([zero, -1.0 * (hit & (lane < half)),
                               hit & (lane >= half) & (lane < ROPE_DIM)], axis=1)
    spread = jnp.asarray(np.concatenate([cos_rows, sin_rows] * 2, axis=0), BF16)
    base = jnp.asarray((lane >= ROPE_DIM).astype(np.float32).reshape(1, LANES))

    def tok(w):
        return pl.BlockSpec((tm, w), lambda i: (i, 0))

    def tok_t(r):
        return pl.BlockSpec((r, tm), lambda i: (0, i))

    outs = pl.pallas_call(
        _proj_kernel,
        grid=(n // tm,),
        in_specs=[tok(d), _full((1, d)), tok_t(1), _full(invf.shape), _full(spread.shape),
                  _full(base.shape),
                  _full(wsb.shape), _full(wrot.shape), _full(wt.shape)],
        out_specs=[tok(3 * wb), tok(wb), tok(wb), tok(2 * LANES), tok(2 * LANES),
                   pl.BlockSpec((tm // LANES, HEAD_DIM, LANES), lambda i: (i, 0, 0)),
                   tok_t(N_HEADS)],
        out_shape=[jax.ShapeDtypeStruct((n, 3 * wb), BF16),
                   jax.ShapeDtypeStruct((n, wb), BF16),
                   jax.ShapeDtypeStruct((n, wb), BF16),
                   jax.ShapeDtypeStruct((n, 2 * LANES), BF16),
                   jax.ShapeDtypeStruct((n, 2 * LANES), BF16),
                   jax.ShapeDtypeStruct((n // LANES, HEAD_DIM, LANES), BF16),
                   jax.ShapeDtypeStruct((N_HEADS, n), F32)],
        compiler_params=_params(1),
        name="mixer_proj",
    )(hf, norm.reshape(1, d), positions.reshape(1, n), invf, spread, base, wsb, wrot, wt)
    return outs, wgate


def _sb_kernel(q_ref, k_ref, v_ref, o_ref, qm_ref, acc_ref, car_ref, *, tq):
    i = pl.program_id(1)
    pairs = W_BRANCH // LANES
    lane = lax.broadcasted_iota(I32, (tq, LANES), 1)
    lo_half = lane < HEAD_DIM
    q = q_ref[0]
    for g in range(pairs):
        qg = q[:, g * LANES:(g + 1) * LANES]
        qm_ref[2 * g] = jnp.where(lo_half, qg, jnp.zeros_like(qg))
        qm_ref[2 * g + 1] = jnp.where(lo_half, jnp.zeros_like(qg), qg)
    suffix = jnp.where(lax.broadcasted_iota(I32, (2 * tq, tq), 0) % tq
                       >= lax.broadcasted_iota(I32, (2 * tq, tq), 1), 1.0, 0.0).astype(BF16)
    strict = (lax.broadcasted_iota(I32, (tq, tq), 1) < lax.broadcasted_iota(I32, (tq, tq), 0))

    acc_ref[...] = jnp.zeros(acc_ref.shape, F32)
    car_ref[...] = jnp.zeros(car_ref.shape, F32)

    def group(j, diag):
        rows = pl.ds(pl.multiple_of(j * tq, tq), tq)
        hl, zs = [], []
        for g in range(pairs):
            kg = k_ref[0, rows, g * LANES:(g + 1) * LANES]
            zz = lax.dot_general(jnp.concatenate([qm_ref[2 * g], qm_ref[2 * g + 1]], axis=0), kg,
                                 _NT, preferred_element_type=F32)
            for hh in range(2):
                z = zz[hh * tq:(hh + 1) * tq]
                neg_abs = lax.bitcast_convert_type(lax.bitcast_convert_type(z, I32) | INT_MIN, F32)
                sp = jnp.maximum(z, 0.0) + jnp.log(1.0 + jnp.exp2(neg_abs)) * LOG2E
                if diag:
                    sp = jnp.where(strict, sp, 0.0)
                hi = sp.astype(BF16)
                lo = (sp - hi.astype(F32)).astype(BF16)
                hl.append(jnp.concatenate([hi, lo], axis=1))
                zs.append(z)
        sums = jnp.dot(jnp.concatenate(hl, axis=0), suffix, preferred_element_type=F32)
        for g in range(pairs):
            probs = []
            for hh in range(2):
                h = 2 * g + hh
                car = car_ref[h]
                blk = sums[h * tq:(h + 1) * tq]
                for l in range(tq // LANES):
                    sl = slice(l * LANES, (l + 1) * LANES)
                    a = jnp.exp2(zs[h][:, sl] - (blk[:, sl] + car))
                    if diag:
                        a = jnp.where(strict[:, sl], a, 0.0)
                    probs.append(a.astype(BF16))
                car_ref[h] = car + jnp.broadcast_to(blk[:, 0:1], (tq, LANES))
            vg = v_ref[0, rows, g * LANES:(g + 1) * LANES]
            vv = jnp.concatenate([jnp.where(lo_half, vg, jnp.zeros_like(vg)),
                                  jnp.where(lo_half, jnp.zeros_like(vg), vg)], axis=0)
            acc_ref[g] += jnp.dot(jnp.concatenate(probs, axis=1), vv, preferred_element_type=F32)

    group(i, True)

    def body(jj, carry):
        group(i - 1 - jj, False)
        return carry

    lax.fori_loop(0, i, body, 0)
    for g in range(pairs):
        o_ref[0, :, g * LANES:(g + 1) * LANES] = acc_ref[g].astype(BF16)


def _sb_attention(qkv, b, s):
    sub = ATT_BLOCK
    tq = 2 * sub if s % (2 * sub) == 0 else sub
    wb = W_BRANCH
    return pl.pallas_call(
        functools.partial(_sb_kernel, tq=tq),
        grid=(b, s // tq),
        in_specs=[pl.BlockSpec((1, tq, wb), lambda bi, i: (bi, i, 0)),
                  pl.BlockSpec((1, s, wb), lambda bi, i: (bi, 0, 1)),
                  pl.BlockSpec((1, s, wb), lambda bi, i: (bi, 0, 2))],
        out_specs=pl.BlockSpec((1, tq, wb), lambda bi, i: (bi, i, 0)),
        out_shape=jax.ShapeDtypeStruct((b, s, wb), BF16),
        scratch_shapes=[pltpu.VMEM((N_HEADS, tq, LANES), BF16),
                        pltpu.VMEM((wb // LANES, tq, LANES), F32),
                        pltpu.VMEM((N_HEADS, tq, LANES), F32)],
        compiler_params=_params(2),
        name="sb_attention",
    )(qkv, qkv, qkv)


def _dsa_kernel(qi_ref, qd_ref, ki_ref, kd_ref, vt_ref, wgt_ref, o_ref,
                key_ref, hi_ref, lo_ref, lom_ref, bias_ref, lg_ref, *, tb, span, n_sel):
    i = pl.program_id(1)
    nspan = ((i + 1) * tb - 1) // span + 1
    srow = lax.broadcasted_iota(I32, (span, tb), 0)
    scol = lax.broadcasted_iota(I32, (span, tb), 1)

    def rows(c):
        return pl.ds(pl.multiple_of(c * span, span), span)

    def span_causal(c):
        return (c * span + srow) <= (i * tb + scol)

    def fold(x, op):
        return op(x.reshape(x.shape[0] // SUBLANES, SUBLANES, x.shape[1]), axis=0)

    def pair_lhs(ref, c):
        kab = ref[0, rows(c), :]
        return jnp.concatenate([kab[:, :LANES], kab[:, LANES:]], axis=0)

    def head_dots(lhs, q_ref):
        out = []
        for gg in range(N_HEADS // 4):
            qq = jnp.concatenate([q_ref[0, :, (2 * gg) * LANES:(2 * gg + 1) * LANES],
                                  q_ref[0, :, (2 * gg + 1) * LANES:(2 * gg + 2) * LANES]], axis=0)
            zz = lax.dot_general(lhs, qq, _NT, preferred_element_type=F32)
            out += [zz[:span, :tb], zz[span:, :tb], zz[:span, tb:], zz[span:, tb:]]
        return out

    def score_span(c, carry):
        lhs = pair_lhs(ki_ref, c)
        score = jnp.zeros((span, tb), F32)
        for h, dots in enumerate(head_dots(lhs, qi_ref)):
            score = score + wgt_ref[h:h + 1, :] * jnp.maximum(dots, 0.0)
        bits = lax.bitcast_convert_type(score, I32)
        bits = jnp.where(bits == INT_MIN, 0, bits)
        key = bits ^ ((bits >> 31) & np.int32(0x7FFFFFFF))
        key = jnp.where(span_causal(c), key, INT_MIN)
        key_ref[rows(c), :] = key
        hi_ref[rows(c), :] = (key >> 16).astype(I16)
        lo_ref[rows(c), :] = ((key & np.int32(0xFFFF)) + INT16_MIN).astype(I16)
        return carry

    lax.fori_loop(0, nspan, score_span, 0)

    def count16(ref, pred):
        n_acc = 8

        def step(c, accs):
            kk = ref[rows(c), :].reshape(span // PACKED_SUBLANES, PACKED_SUBLANES, tb)
            accs = list(accs)
            for r in range(span // PACKED_SUBLANES):
                a = accs[r % n_acc]
                accs[r % n_acc] = jnp.where(pred(kk[r]), a + np.int16(1), a)
            return tuple(accs)

        accs = lax.fori_loop(0, nspan, step, (jnp.zeros((PACKED_SUBLANES, tb), I16),) * n_acc)
        return jnp.sum(sum(accs[1:], accs[0]).astype(I32), axis=0, keepdims=True)

    def packed(x):
        return jnp.broadcast_to(x, (PACKED_SUBLANES, tb)).astype(I16)

    def search16(ref, base, state):
        def refine(cand, state):
            thr, cnt = state
            cand16 = packed(cand)
            c = base + count16(ref, lambda kk: kk >= cand16)
            ok = c >= n_sel
            return jnp.where(ok, cand, thr), jnp.where(ok, c, cnt)

        state = refine(jnp.zeros((1, tb), I32), state)
        return lax.fori_loop(
            0, 15, lambda it, st: refine(st[0] + jnp.left_shift(np.int32(1), 14 - it), st), state)

    lowest = jnp.full((1, tb), INT16_MIN, I32)
    thr_hi, cnt = search16(hi_ref, 0, (lowest, jnp.zeros((1, tb), I32) + nspan * span))
    thr_hi16 = packed(thr_hi)

    def match_span(c, carry):
        hi = hi_ref[rows(c), :].reshape(span // PACKED_SUBLANES, PACKED_SUBLANES, tb)
        lo = lo_ref[rows(c), :].reshape(span // PACKED_SUBLANES, PACKED_SUBLANES, tb)
        lom_ref[rows(c), :] = jnp.where(hi == thr_hi16[None], lo, np.int16(INT16_MIN)
                                        ).reshape(span, tb)
        return carry

    lax.fori_loop(0, nspan, match_span, 0)
    n_above = count16(hi_ref, lambda kk: kk > thr_hi16)
    thr_lo, cnt = search16(lom_ref, n_above, (lowest, cnt))
    thr = thr_hi * 65536 + (thr_lo - INT16_MIN)

    def plain_bias(c, carry):
        sel = (key_ref[rows(c), :] >= thr) & span_causal(c)
        bias_ref[rows(c), :] = jnp.where(sel, 0.0, -jnp.inf)
        return carry

    lax.fori_loop(0, nspan, plain_bias, 0)

    tie = (cnt > n_sel) & (thr > INT_MIN)

    @pl.when(jnp.max(tie.astype(I32)) > 0)
    def _():
        def count_gt(c, acc):
            return acc + fold(jnp.where(key_ref[rows(c), :] > thr, 1, 0).astype(I32), jnp.sum)
        n_gt = jnp.sum(lax.fori_loop(0, nspan, count_gt, jnp.zeros((SUBLANES, tb), I32)),
                       axis=0, keepdims=True)
        need = jnp.where(tie, (n_sel - n_gt).astype(F32), 3.0e38)
        before = jnp.where(lax.broadcasted_iota(I32, (span, span), 1)
                           < lax.broadcasted_iota(I32, (span, span), 0),
                           1.0, 0.0).astype(BF16)

        def tie_bias(c, seen):
            kk = key_ref[rows(c), :]
            eq = kk == thr
            eqf = jnp.where(eq, 1.0, 0.0)
            rank = jnp.dot(before, eqf.astype(BF16), preferred_element_type=F32) + seen
            sel = ((kk > thr) | (eq & (rank < need))) & span_causal(c)
            bias_ref[rows(c), :] = jnp.where(sel, 0.0, -jnp.inf)
            return seen + jnp.sum(eqf, axis=0, keepdims=True)

        lax.fori_loop(0, nspan, tie_bias, jnp.zeros((1, tb), F32))

    def logits_span(c, mx):
        lhs = pair_lhs(kd_ref, c)
        bias = bias_ref[rows(c), :]
        new = []
        for h, dots in enumerate(head_dots(lhs, qd_ref)):
            lg = dots + bias
            lg_ref[h, rows(c), :] = lg
            new.append(jnp.maximum(mx[h], fold(lg, jnp.max)))
        return tuple(new)

    ninf = jnp.full((SUBLANES, tb), -jnp.inf, F32)
    mx = lax.fori_loop(0, nspan, logits_span, (ninf,) * N_HEADS)
    mx = tuple(jnp.max(m, axis=0, keepdims=True) for m in mx)

    def pv_span(c, outs):
        vt = jnp.concatenate([vt_ref[0, c * (span // LANES) + k] for k in range(span // LANES)],
                             axis=1)
        vt1 = jnp.concatenate([vt, jnp.ones((PACKED_SUBLANES, span), BF16)], axis=0)
        new = []
        for h in range(N_HEADS):
            p = jnp.exp2(lg_ref[h, rows(c), :] - mx[h])
            new.append(outs[h] + jnp.dot(vt1, p.astype(BF16), preferred_element_type=F32))
        return tuple(new)

    zero = jnp.zeros((HEAD_DIM + PACKED_SUBLANES, tb), F32)
    res = lax.fori_loop(0, nspan, pv_span, (zero,) * N_HEADS)
    outs = [res[h][:HEAD_DIM] / res[h][HEAD_DIM:HEAD_DIM + 1] for h in range(N_HEADS)]
    for g in range(N_HEADS // 2):
        pair = jnp.concatenate([outs[2 * g], outs[2 * g + 1]], axis=0)
        o_ref[0, :, g * LANES:(g + 1) * LANES] = pair.T.astype(BF16)


def _dsa_attention(qd, qi, kd, ki, vt, wgt, b, s):
    tb = DSA_QUERIES
    span = DSA_SPAN
    assert s % span == 0 and span % tb == 0, s
    nq = s // tb
    n_sel = min(TOPK_MAX, s // 4)
    return pl.pallas_call(
        functools.partial(_dsa_kernel, tb=tb, span=span, n_sel=n_sel),
        grid=(b, nq),
        in_specs=[pl.BlockSpec((1, tb, W_BRANCH), lambda bi, i: (bi, i, 0)),
                  pl.BlockSpec((1, tb, W_BRANCH), lambda bi, i: (bi, i, 0)),
                  pl.BlockSpec((1, s, 2 * LANES), lambda bi, i: (bi, 0, 0)),
                  pl.BlockSpec((1, s, 2 * LANES), lambda bi, i: (bi, 0, 0)),
                  pl.BlockSpec((1, s // LANES, HEAD_DIM, LANES), lambda bi, i: (bi, 0, 0, 0)),
                  pl.BlockSpec((N_HEADS, tb), lambda bi, i: (0, bi * nq + i))],
        out_specs=pl.BlockSpec((1, tb, W_BRANCH), lambda bi, i: (bi, i, 0)),
        out_shape=jax.ShapeDtypeStruct((b, s, W_BRANCH), BF16),
        scratch_shapes=[pltpu.VMEM((s, tb), I32), pltpu.VMEM((s, tb), I16),
                        pltpu.VMEM((s, tb), I16), pltpu.VMEM((s, tb), I16),
                        pltpu.VMEM((s, tb), F32),
                        pltpu.VMEM((N_HEADS, s, tb), F32)],
        compiler_params=_params(2),
        name="dsa_attention",
    )(qi, qd, ki, kd, vt, wgt)


def _merge_kernel(h_ref, g_ref, ysb_ref, yds_ref, wg_ref, wosb_ref, wods_ref, wo_ref, o_ref):
    h = h_ref[...]
    d = h.shape[1]
    u = _rmsnorm(h, g_ref[...]).astype(BF16)
    gates = jax.nn.sigmoid(jnp.dot(u, wg_ref[...], preferred_element_type=F32))
    y_sb = jnp.dot(ysb_ref[...], wosb_ref[...], preferred_element_type=F32)
    y_ds = jnp.dot(yds_ref[...], wods_ref[...], preferred_element_type=F32)
    merged = gates[:, :d] * y_sb + gates[:, d:] * y_ds
    o_ref[...] = h + jnp.dot(merged.astype(BF16), wo_ref[...], preferred_element_type=F32)


def _merge_out(hf, norm, y_sb, y_dsa, wgate, w_out_sb, w_out_dsa, w_out):
    n, d = hf.shape
    tm = _token_tile(n)
    wb = W_BRANCH

    def tok(w):
        return pl.BlockSpec((tm, w), lambda i: (i, 0))

    return pl.pallas_call(
        _merge_kernel,
        grid=(n // tm,),
        in_specs=[tok(d), _full((1, d)), tok(wb), tok(wb), _full(wgate.shape),
                  _full((wb, d)), _full((wb, d)), _full((d, d))],
        out_specs=tok(d),
        out_shape=jax.ShapeDtypeStruct((n, d), F32),
        compiler_params=_params(1),
        name="merge_out",
    )(hf, norm.reshape(1, d), y_sb, y_dsa, wgate,
      w_out_sb.astype(BF16), w_out_dsa.astype(BF16), w_out.astype(BF16))


def kernel(x, p, positions, ffn1_norm, ffn1_w1, ffn1_w2, mix_norm, w_in, w_out_sb, w_out_dsa,
           w_out, ffn2_norm, ffn2_w1, ffn2_w2, ple_norm, ple_w_gate, ple_w_proj, final_norm):
    b, s, d = x.shape
    n = b * s
    depth = p.shape[0]
    nq = s // ATT_BLOCK
    h = x.reshape(n, d)
    for i in range(depth):
        h = _ffn_half_step(h, ffn1_norm[i], ffn1_w1[i], ffn1_w2[i])
        (sb, qd, qi, kd, ki, vt, wgt), wgate = _mixer_proj(h, positions, mix_norm[i], w_in[i])
        y_sb = _sb_attention(sb.reshape(b, s, 3 * W_BRANCH), b, s)
        vt4 = vt.reshape(b, nq, HEAD_DIM, ATT_BLOCK)
        y_dsa = _dsa_attention(qd.reshape(b, s, W_BRANCH), qi.reshape(b, s, W_BRANCH),
                               kd.reshape(b, s, 2 * LANES), ki.reshape(b, s, 2 * LANES),
                               vt4, wgt, b, s)
        h = _merge_out(h, mix_norm[i], y_sb.reshape(n, W_BRANCH), y_dsa.reshape(n, W_BRANCH), wgate,
                       w_out_sb[i], w_out_dsa[i], w_out[i])
        h = _ffn_ple_step(h, ffn2_norm[i], ffn2_w1[i], ffn2_w2[i], ple_norm[i],
                          p[i].reshape(n, -1), ple_w_gate[i], ple_w_proj[i], final_norm,
                          final=(i == depth - 1))
    return h.reshape(b, s, d)
```

```python
import functools

import numpy as np
import jax
import jax.numpy as jnp
from jax import lax
from jax.experimental import pallas as pl
from jax.experimental.pallas import tpu as pltpu

F32 = jnp.float32
BF16 = jnp.bfloat16
I32 = jnp.int32
I16 = jnp.int16

HEAD_DIM = 64
N_HEADS = 8
W_BRANCH = N_HEADS * HEAD_DIM
TOPK_MAX = 256
ROPE_THETA = 500000.0
ROPE_DIM = HEAD_DIM // 4
EPS = 1e-6
LOG2E = 1.4426950408889634

LANES = 128
SUBLANES = 8
ATT_BLOCK = 128
DSA_QUERIES = 512
DSA_SPAN = 512
PACKED_SUBLANES = 16
INT_MIN = np.int32(-2**31)
INT16_MIN = -2**15
VMEM_LIMIT = 56 * 1024 * 1024

_NT = (((1,), (1,)), ((), ()))


def _rmsnorm(x, g):
    ms = jnp.mean(x * x, axis=-1, keepdims=True)
    return x * lax.rsqrt(ms + EPS) * g


def _token_tile(n):
    for tm in (512, 256, 128):
        if n % tm == 0:
            return tm
    raise ValueError(f"token count {n} must be a multiple of 128")


def _ff_chunks(d_ff):
    assert d_ff % 256 == 0, d_ff
    chunks, left = [], d_ff
    while left:
        c = min(768, left)
        chunks.append(c)
        left -= c
    return tuple(chunks)


def _params(n_axes):
    return pltpu.CompilerParams(dimension_semantics=("arbitrary",) * n_axes,
                                vmem_limit_bytes=VMEM_LIMIT)


def _full(shape):
    nd = len(shape)
    return pl.BlockSpec(shape, lambda *_: (0,) * nd)


def _swiglu_half_step(x, g_ref, w1a_ref, w1b_ref, w2_ref, chunks):
    xn = _rmsnorm(x, g_ref[...]).astype(BF16)
    acc = jnp.zeros(x.shape, F32)
    off = 0
    for cw in chunks:
        a = jnp.dot(xn, w1a_ref[:, off:off + cw], preferred_element_type=F32)
        b = jnp.dot(xn, w1b_ref[:, off:off + cw], preferred_element_type=F32)
        hid = (a * jax.nn.sigmoid(a) * b).astype(BF16)
        acc = acc + jnp.dot(hid, w2_ref[off:off + cw, :], preferred_element_type=F32)
        off += cw
    return x + 0.5 * acc


def _ffn_kernel(x_ref, g_ref, w1a_ref, w1b_ref, w2_ref, o_ref, *, chunks):
    o_ref[...] = _swiglu_half_step(x_ref[...], g_ref, w1a_ref, w1b_ref, w2_ref, chunks)


def _ffn_ple_kernel(x_ref, g_ref, w1a_ref, w1b_ref, w2_ref, gp_ref, p_ref, wg_ref, wp_ref,
                    gf_ref, o_ref, *, chunks, final):
    h = _swiglu_half_step(x_ref[...], g_ref, w1a_ref, w1b_ref, w2_ref, chunks)
    u = _rmsnorm(h, gp_ref[...]).astype(BF16)
    gate = jax.nn.sigmoid(jnp.dot(u, wg_ref[...], preferred_element_type=F32))
    emb = jnp.dot(p_ref[...].astype(BF16), wp_ref[...], preferred_element_type=F32)
    h = h + gate * emb
    o_ref[...] = _rmsnorm(h, gf_ref[...]) if final else h


def _ffn_weights(w1, w2):
    d_ff = w2.shape[0]
    return w1[:, :d_ff].astype(BF16), w1[:, d_ff:].astype(BF16), w2.astype(BF16)


def _ffn_half_step(xf, norm, w1, w2):
    n, d = xf.shape
    tm = _token_tile(n)
    w1a, w1b, w2b = _ffn_weights(w1, w2)
    tok = pl.BlockSpec((tm, d), lambda i: (i, 0))
    return pl.pallas_call(
        functools.partial(_ffn_kernel, chunks=_ff_chunks(w2.shape[0])),
        grid=(n // tm,),
        in_specs=[tok, _full((1, d)), _full(w1a.shape), _full(w1b.shape), _full(w2b.shape)],
        out_specs=tok,
        out_shape=jax.ShapeDtypeStruct((n, d), F32),
        compiler_params=_params(1),
        name="ffn_half_step",
    )(xf, norm.reshape(1, d), w1a, w1b, w2b)


def _ffn_ple_step(xf, norm, w1, w2, ple_norm, pf, ple_w_gate, ple_w_proj, final_norm, final):
    n, d = xf.shape
    dp = pf.shape[1]
    tm = _token_tile(n)
    w1a, w1b, w2b = _ffn_weights(w1, w2)

    def tok(w):
        return pl.BlockSpec((tm, w), lambda i: (i, 0))

    return pl.pallas_call(
        functools.partial(_ffn_ple_kernel, chunks=_ff_chunks(w2.shape[0]), final=final),
        grid=(n // tm,),
        in_specs=[tok(d), _full((1, d)), _full(w1a.shape), _full(w1b.shape), _full(w2b.shape),
                  _full((1, d)), tok(dp), _full((d, d)), _full((dp, d)), _full((1, d))],
        out_specs=tok(d),
        out_shape=jax.ShapeDtypeStruct((n, d), F32),
        compiler_params=_params(1),
        name="ffn_ple_step",
    )(xf, norm.reshape(1, d), w1a, w1b, w2b, ple_norm.reshape(1, d), pf,
      ple_w_gate.astype(BF16), ple_w_proj.astype(BF16), final_norm.reshape(1, d))


def _proj_kernel(h_ref, g_ref, pos_ref, invf_ref, spread_ref, base_ref, wsb_ref, wrot_ref, wt_ref,
                 sb_ref, qd_ref, qi_ref, kd_ref, ki_ref, vt_ref, wgt_ref):
    u = _rmsnorm(h_ref[...], g_ref[...]).astype(BF16)
    sb_ref[...] = jnp.dot(u, wsb_ref[...], preferred_element_type=F32).astype(BF16)

    ang = invf_ref[...] * pos_ref[...].astype(F32)
    cs = jnp.concatenate([jnp.cos(ang), jnp.sin(ang)], axis=0)
    cs_hi = cs.astype(BF16)
    cs_lo = (cs - cs_hi.astype(F32)).astype(BF16)
    pat = lax.dot_general(jnp.concatenate([cs_hi, cs_lo], axis=0), spread_ref[...],
                          (((0,), (0,)), ((), ())), preferred_element_type=F32)
    half = ROPE_DIM // 2
    cos = pat[:, :LANES] + base_ref[...]
    s_lo = pat[:, LANES:2 * LANES]
    s_hi = pat[:, 2 * LANES:]
    rot = jnp.dot(u, wrot_ref[...], preferred_element_type=F32)
    outs = ((qd_ref, 0, W_BRANCH), (qi_ref, W_BRANCH, W_BRANCH),
            (kd_ref, 2 * W_BRANCH, 2 * LANES), (ki_ref, 2 * W_BRANCH + 2 * LANES, 2 * LANES))
    for ref, base, width in outs:
        for g in range(width // LANES):
            xg = rot[:, base + g * LANES: base + (g + 1) * LANES]
            yg = (xg * cos + pltpu.roll(xg, LANES - half, 1) * s_lo
                  + pltpu.roll(xg, half, 1) * s_hi)
            ref[:, g * LANES:(g + 1) * LANES] = yg.astype(BF16)

    tr = lax.dot_general(wt_ref[...], u, _NT, preferred_element_type=F32)
    for kb in range(vt_ref.shape[0]):
        vt_ref[kb] = tr[:HEAD_DIM, kb * LANES:(kb + 1) * LANES].astype(BF16)
    wgt_ref[...] = tr[HEAD_DIM:] * (N_HEADS ** -0.5)


def _mixer_proj(hf, positions, norm, w_in):
    n, d = hf.shape
    tm = _token_tile(n)
    wb = W_BRANCH
    o = 0
    cols = {}
    for name, size in (("q_sb", wb), ("k_sb", wb), ("v_sb", wb), ("q_d", wb), ("k_d", HEAD_DIM),
                       ("v_d", HEAD_DIM), ("q_i", wb), ("k_i", HEAD_DIM), ("w_i", N_HEADS),
                       ("g_sb", d), ("g_dsa", d)):
        cols[name] = w_in[:, o:o + size]
        o += size
    scale = HEAD_DIM ** -0.5
    zeros = jnp.zeros((d, HEAD_DIM), w_in.dtype)
    wsb = jnp.concatenate([cols["q_sb"] * (scale * LOG2E), cols["k_sb"], cols["v_sb"]],
                          axis=1).astype(BF16)
    wrot = jnp.concatenate([cols["q_d"] * (scale * LOG2E), cols["q_i"] * scale,
                            cols["k_d"], zeros, zeros, cols["k_d"],
                            cols["k_i"], zeros, zeros, cols["k_i"]], axis=1).astype(BF16)
    wt = jnp.concatenate([cols["v_d"], cols["w_i"]], axis=1).T.astype(BF16)
    wgate = jnp.concatenate([cols["g_sb"], cols["g_dsa"]], axis=1).astype(BF16)

    half = ROPE_DIM // 2
    invf = (ROPE_THETA ** (-jnp.arange(0, ROPE_DIM, 2, dtype=F32) / ROPE_DIM)).reshape(half, 1)
    lane = np.arange(LANES) % HEAD_DIM
    hit = (lane[None, :] % half == np.arange(half)[:, None])
    zero = np.zeros((half, LANES), np.float32)
    cos_rows = np.concatenate([hit & (lane < ROPE_DIM), zero, zero], axis=1)
    sin_rows = np.concatenate([zero, -1.0 * (hit & (lane < half)),
                               hit & (lane >= half) & (lane < ROPE_DIM)], axis=1)
    spread = jnp.asarray(np.concatenate([cos_rows, sin_rows] * 2, axis=0), BF16)
    base = jnp.asarray((lane >= ROPE_DIM).astype(np.float32).reshape(1, LANES))

    def tok(w):
        return pl.BlockSpec((tm, w), lambda i: (i, 0))

    def tok_t(r):
        return pl.BlockSpec((r, tm), lambda i: (0, i))

    outs = pl.pallas_call(
        _proj_kernel,
        grid=(n // tm,),
        in_specs=[tok(d), _full((1, d)), tok_t(1), _full(invf.shape), _full(spread.shape),
                  _full(base.shape),
                  _full(wsb.shape), _full(wrot.shape), _full(wt.shape)],
        out_specs=[tok(3 * wb), tok(wb), tok(wb), tok(2 * LANES), tok(2 * LANES),
                   pl.BlockSpec((tm // LANES, HEAD_DIM, LANES), lambda i: (i, 0, 0)),
                   tok_t(N_HEADS)],
        out_shape=[jax.ShapeDtypeStruct((n, 3 * wb), BF16),
                   jax.ShapeDtypeStruct((n, wb), BF16),
                   jax.ShapeDtypeStruct((n, wb), BF16),
                   jax.ShapeDtypeStruct((n, 2 * LANES), BF16),
                   jax.ShapeDtypeStruct((n, 2 * LANES), BF16),
                   jax.ShapeDtypeStruct((n // LANES, HEAD_DIM, LANES), BF16),
                   jax.ShapeDtypeStruct((N_HEADS, n), F32)],
        compiler_params=_params(1),
        name="mixer_proj",
    )(hf, norm.reshape(1, d), positions.reshape(1, n), invf, spread, base, wsb, wrot, wt)
    return outs, wgate


def _sb_kernel(q_ref, k_ref, v_ref, o_ref, qm_ref, acc_ref, car_ref, *, tq, nb):
    i = pl.program_id(1)
    pairs = W_BRANCH // LANES
    lane = lax.broadcasted_iota(I32, (tq, LANES), 1)
    lo_half = lane < HEAD_DIM
    for bb in range(nb):
        q = q_ref[bb]
        for g in range(pairs):
            qg = q[:, g * LANES:(g + 1) * LANES]
            qm_ref[bb * N_HEADS + 2 * g] = jnp.where(lo_half, qg, jnp.zeros_like(qg))
            qm_ref[bb * N_HEADS + 2 * g + 1] = jnp.where(lo_half, jnp.zeros_like(qg), qg)
    suffix = jnp.where(lax.broadcasted_iota(I32, (2 * tq, tq), 0) % tq
                       >= lax.broadcasted_iota(I32, (2 * tq, tq), 1), 1.0, 0.0).astype(BF16)
    strict = (lax.broadcasted_iota(I32, (tq, tq), 1) < lax.broadcasted_iota(I32, (tq, tq), 0))

    acc_ref[...] = jnp.zeros(acc_ref.shape, F32)
    car_ref[...] = jnp.zeros(car_ref.shape, F32)

    def group(j, diag):
        rows = pl.ds(pl.multiple_of(j * tq, tq), tq)
        hl, zs = [], []
        for bb in range(nb):
            for g in range(pairs):
                kg = k_ref[bb, rows, g * LANES:(g + 1) * LANES]
                qq = jnp.concatenate([qm_ref[bb * N_HEADS + 2 * g],
                                      qm_ref[bb * N_HEADS + 2 * g + 1]], axis=0)
                zz = lax.dot_general(qq, kg, _NT, preferred_element_type=F32)
                for hh in range(2):
                    z = zz[hh * tq:(hh + 1) * tq]
                    neg_abs = lax.bitcast_convert_type(
                        lax.bitcast_convert_type(z, I32) | INT_MIN, F32)
                    sp = jnp.maximum(z, 0.0) + jnp.log(1.0 + jnp.exp2(neg_abs)) * LOG2E
                    if diag:
                        sp = jnp.where(strict, sp, 0.0)
                    hi = sp.astype(BF16)
                    lo = (sp - hi.astype(F32)).astype(BF16)
                    hl.append(jnp.concatenate([hi, lo], axis=1))
                    zs.append(z)
        sums = jnp.dot(jnp.concatenate(hl, axis=0), suffix, preferred_element_type=F32)
        for bb in range(nb):
            for g in range(pairs):
                probs = []
                for hh in range(2):
                    h = bb * N_HEADS + 2 * g + hh
                    car = car_ref[h]
                    blk = sums[h * tq:(h + 1) * tq]
                    for l in range(tq // LANES):
                        sl = slice(l * LANES, (l + 1) * LANES)
                        a = jnp.exp2(zs[h][:, sl] - (blk[:, sl] + car))
                        if diag:
                            a = jnp.where(strict[:, sl], a, 0.0)
                        probs.append(a.astype(BF16))
                    car_ref[h] = car + jnp.broadcast_to(blk[:, 0:1], (tq, LANES))
                vg = v_ref[bb, rows, g * LANES:(g + 1) * LANES]
                vv = jnp.concatenate([jnp.where(lo_half, vg, jnp.zeros_like(vg)),
                                      jnp.where(lo_half, jnp.zeros_like(vg), vg)], axis=0)
                acc_ref[bb * pairs + g] += jnp.dot(jnp.concatenate(probs, axis=1), vv,
                                                   preferred_element_type=F32)

    group(i, True)

    def body(jj, carry):
        group(i - 1 - jj, False)
        return carry

    lax.fori_loop(0, i, body, 0)
    for bb in range(nb):
        for g in range(pairs):
            o_ref[bb, :, g * LANES:(g + 1) * LANES] = acc_ref[bb * pairs + g].astype(BF16)


def _sb_attention(qkv, b, s):
    sub = ATT_BLOCK
    tq = 2 * sub if s % (2 * sub) == 0 else sub
    nb = 2 if b % 2 == 0 else 1
    wb = W_BRANCH
    return pl.pallas_call(
        functools.partial(_sb_kernel, tq=tq, nb=nb),
        grid=(b // nb, s // tq),
        in_specs=[pl.BlockSpec((nb, tq, wb), lambda bi, i: (bi, i, 0)),
                  pl.BlockSpec((nb, s, wb), lambda bi, i: (bi, 0, 1)),
                  pl.BlockSpec((nb, s, wb), lambda bi, i: (bi, 0, 2))],
        out_specs=pl.BlockSpec((nb, tq, wb), lambda bi, i: (bi, i, 0)),
        out_shape=jax.ShapeDtypeStruct((b, s, wb), BF16),
        scratch_shapes=[pltpu.VMEM((nb * N_HEADS, tq, LANES), BF16),
                        pltpu.VMEM((nb * wb // LANES, tq, LANES), F32),
                        pltpu.VMEM((nb * N_HEADS, tq, LANES), F32)],
        compiler_params=_params(2),
        name="sb_attention",
    )(qkv, qkv, qkv)


def _dsa_kernel(qi_ref, qd_ref, ki_ref, kd_ref, vt_ref, wgt_ref, o_ref,
                key_ref, hi_ref, lo_ref, lom_ref, bias_ref, lg_ref, *, tb, span, n_sel):
    i = pl.program_id(1)
    nspan = ((i + 1) * tb - 1) // span + 1
    srow = lax.broadcasted_iota(I32, (span, tb), 0)
    scol = lax.broadcasted_iota(I32, (span, tb), 1)

    def rows(c):
        return pl.ds(pl.multiple_of(c * span, span), span)

    def span_causal(c):
        return (c * span + srow) <= (i * tb + scol)

    def fold(x, op):
        return op(x.reshape(x.shape[0] // SUBLANES, SUBLANES, x.shape[1]), axis=0)

    def pair_lhs(ref, c):
        kab = ref[0, rows(c), :]
        n16 = span // PACKED_SUBLANES
        return jnp.stack([kab[:, :LANES].reshape(n16, PACKED_SUBLANES, LANES),
                          kab[:, LANES:].reshape(n16, PACKED_SUBLANES, LANES)],
                         axis=1).reshape(2 * span, LANES)

    def head_dots(lhs, q_ref, gg):
        qq = jnp.concatenate([q_ref[0, :, (2 * gg) * LANES:(2 * gg + 1) * LANES],
                              q_ref[0, :, (2 * gg + 1) * LANES:(2 * gg + 2) * LANES]], axis=0)
        zz = lax.dot_general(lhs, qq, _NT, preferred_element_type=F32)
        zz = zz.reshape(span // PACKED_SUBLANES, 2, PACKED_SUBLANES, 2 * tb)
        even, odd = zz[:, 0].reshape(span, 2 * tb), zz[:, 1].reshape(span, 2 * tb)
        return [even[:, :tb], odd[:, :tb], even[:, tb:], odd[:, tb:]]

    def score_span(c, carry):
        lhs = pair_lhs(ki_ref, c)
        score = jnp.zeros((span, tb), F32)
        for gg in range(N_HEADS // 4):
            for hh, dots in enumerate(head_dots(lhs, qi_ref, gg)):
                h = 4 * gg + hh
                score = score + wgt_ref[h:h + 1, :] * jnp.maximum(dots, 0.0)
        bits = lax.bitcast_convert_type(score, I32)
        bits = jnp.where(bits == INT_MIN, 0, bits)
        key = bits ^ ((bits >> 31) & np.int32(0x7FFFFFFF))
        key = jnp.where(span_causal(c), key, INT_MIN)
        key_ref[rows(c), :] = key
        hi_ref[rows(c), :] = (key >> 16).astype(I16)
        lo_ref[rows(c), :] = ((key & np.int32(0xFFFF)) + INT16_MIN).astype(I16)
        return carry

    lax.fori_loop(0, nspan, score_span, 0)

    def count16(ref, pred):
        n_acc = 8

        def scan(c, accs):
            kk = ref[rows(c), :].reshape(span // PACKED_SUBLANES, PACKED_SUBLANES, tb)
            accs = list(accs)
            for r in range(span // PACKED_SUBLANES):
                a = accs[r % n_acc]
                accs[r % n_acc] = jnp.where(pred(kk[r]), a + np.int16(1), a)
            return tuple(accs)

        accs = lax.fori_loop(0, nspan, scan, (jnp.zeros((PACKED_SUBLANES, tb), I16),) * n_acc)
        return jnp.sum(sum(accs[1:], accs[0]).astype(I32), axis=0, keepdims=True)

    def packed(x):
        return jnp.broadcast_to(x, (PACKED_SUBLANES, tb)).astype(I16)

    def search16(ref, base, state):
        def refine(cand, state):
            thr, cnt = state
            cand16 = packed(cand)
            c = base + count16(ref, lambda kk: kk >= cand16)
            ok = c >= n_sel
            return jnp.where(ok, cand, thr), jnp.where(ok, c, cnt)

        state = refine(jnp.zeros((1, tb), I32), state)
        return lax.fori_loop(
            0, 15, lambda it, st: refine(st[0] + jnp.left_shift(np.int32(1), 14 - it), st), state)

    lowest = jnp.full((1, tb), INT16_MIN, I32)
    thr_hi, cnt = search16(hi_ref, 0, (lowest, jnp.zeros((1, tb), I32) + nspan * span))
    thr_hi16 = packed(thr_hi)

    def match_span(c, carry):
        hi = hi_ref[rows(c), :].reshape(span // PACKED_SUBLANES, PACKED_SUBLANES, tb)
        lo = lo_ref[rows(c), :].reshape(span // PACKED_SUBLANES, PACKED_SUBLANES, tb)
        lom_ref[rows(c), :] = jnp.where(hi == thr_hi16[None], lo, np.int16(INT16_MIN)
                                        ).reshape(span, tb)
        return carry

    lax.fori_loop(0, nspan, match_span, 0)
    n_above = count16(hi_ref, lambda kk: kk > thr_hi16)
    thr_lo, cnt = search16(lom_ref, n_above, (lowest, cnt))
    thr = thr_hi * 65536 + (thr_lo - INT16_MIN)

    def plain_bias(c, carry):
        sel = (key_ref[rows(c), :] >= thr) & span_causal(c)
        bias_ref[rows(c), :] = jnp.where(sel, 0.0, -jnp.inf)
        return carry

    lax.fori_loop(0, nspan, plain_bias, 0)

    tie = (cnt > n_sel) & (thr > INT_MIN)

    @pl.when(jnp.max(tie.astype(I32)) > 0)
    def _():
        def count_gt(c, acc):
            return acc + fold(jnp.where(key_ref[rows(c), :] > thr, 1, 0).astype(I32), jnp.sum)
        n_gt = jnp.sum(lax.fori_loop(0, nspan, count_gt, jnp.zeros((SUBLANES, tb), I32)),
                       axis=0, keepdims=True)
        need = jnp.where(tie, (n_sel - n_gt).astype(F32), 3.0e38)
        before = jnp.where(lax.broadcasted_iota(I32, (span, span), 1)
                           < lax.broadcasted_iota(I32, (span, span), 0),
                           1.0, 0.0).astype(BF16)

        def tie_bias(c, seen):
            kk = key_ref[rows(c), :]
            eq = kk == thr
            eqf = jnp.where(eq, 1.0, 0.0)
            rank = jnp.dot(before, eqf.astype(BF16), preferred_element_type=F32) + seen
            sel = ((kk > thr) | (eq & (rank < need))) & span_causal(c)
            bias_ref[rows(c), :] = jnp.where(sel, 0.0, -jnp.inf)
            return seen + jnp.sum(eqf, axis=0, keepdims=True)

        lax.fori_loop(0, nspan, tie_bias, jnp.zeros((1, tb), F32))

    ninf = jnp.full((SUBLANES, tb), -jnp.inf, F32)
    zero = jnp.zeros((HEAD_DIM + PACKED_SUBLANES, tb), F32)
    for gg in range(N_HEADS // 4):
        def logits_span(c, mx):
            bias = bias_ref[rows(c), :]
            new = []
            for h, dots in enumerate(head_dots(pair_lhs(kd_ref, c), qd_ref, gg)):
                lg = dots + bias
                lg_ref[h, rows(c), :] = lg
                new.append(jnp.maximum(mx[h], fold(lg, jnp.max)))
            return tuple(new)

        mx = lax.fori_loop(0, nspan, logits_span, (ninf,) * 4)
        mx = tuple(jnp.max(m, axis=0, keepdims=True) for m in mx)

        def pv_span(c, outs):
            vt = jnp.concatenate([vt_ref[0, c * (span // LANES) + k]
                                  for k in range(span // LANES)], axis=1)
            vt1 = jnp.concatenate([vt, jnp.ones((PACKED_SUBLANES, span), BF16)], axis=0)
            new = []
            for h in range(4):
                p = jnp.exp2(lg_ref[h, rows(c), :] - mx[h])
                new.append(outs[h] + jnp.dot(vt1, p.astype(BF16), preferred_element_type=F32))
            return tuple(new)

        res = lax.fori_loop(0, nspan, pv_span, (zero,) * 4)
        outs = [res[h][:HEAD_DIM] / res[h][HEAD_DIM:HEAD_DIM + 1] for h in range(4)]
        for g in range(2):
            pair = jnp.concatenate([outs[2 * g], outs[2 * g + 1]], axis=0)
            o_ref[0, :, (2 * gg + g) * LANES:(2 * gg + g + 1) * LANES] = pair.T.astype(BF16)


def _dsa_attention(qd, qi, kd, ki, vt, wgt, b, s):
    tb = DSA_QUERIES
    span = DSA_SPAN
    assert s % span == 0 and span % tb == 0, s
    nq = s // tb
    n_sel = min(TOPK_MAX, s // 4)
    return pl.pallas_call(
        functools.partial(_dsa_kernel, tb=tb, span=span, n_sel=n_sel),
        grid=(b, nq),
        in_specs=[pl.BlockSpec((1, tb, W_BRANCH), lambda bi, i: (bi, i, 0)),
                  pl.BlockSpec((1, tb, W_BRANCH), lambda bi, i: (bi, i, 0)),
                  pl.BlockSpec((1, s, 2 * LANES), lambda bi, i: (bi, 0, 0)),
                  pl.BlockSpec((1, s, 2 * LANES), lambda bi, i: (bi, 0, 0)),
                  pl.BlockSpec((1, s // LANES, HEAD_DIM, LANES), lambda bi, i: (bi, 0, 0, 0)),
                  pl.BlockSpec((N_HEADS, tb), lambda bi, i: (0, bi * nq + i))],
        out_specs=pl.BlockSpec((1, tb, W_BRANCH), lambda bi, i: (bi, i, 0)),
        out_shape=jax.ShapeDtypeStruct((b, s, W_BRANCH), BF16),
        scratch_shapes=[pltpu.VMEM((s, tb), I32), pltpu.VMEM((s, tb), I16),
                        pltpu.VMEM((s, tb), I16), pltpu.VMEM((s, tb), I16),
                        pltpu.VMEM((s, tb), F32),
                        pltpu.VMEM((4, s, tb), F32)],
        compiler_params=_params(2),
        name="dsa_attention",
    )(qi, qd, ki, kd, vt, wgt)


def _merge_kernel(h_ref, g_ref, ysb_ref, yds_ref, wg_ref, wosb_ref, wods_ref, wo_ref, o_ref):
    h = h_ref[...]
    d = h.shape[1]
    u = _rmsnorm(h, g_ref[...]).astype(BF16)
    gates = jax.nn.sigmoid(jnp.dot(u, wg_ref[...], preferred_element_type=F32))
    y_sb = jnp.dot(ysb_ref[...], wosb_ref[...], preferred_element_type=F32)
    y_ds = jnp.dot(yds_ref[...], wods_ref[...], preferred_element_type=F32)
    merged = gates[:, :d] * y_sb + gates[:, d:] * y_ds
    o_ref[...] = h + jnp.dot(merged.astype(BF16), wo_ref[...], preferred_element_type=F32)


def _merge_out(hf, norm, y_sb, y_dsa, wgate, w_out_sb, w_out_dsa, w_out):
    n, d = hf.shape
    tm = _token_tile(n)
    wb = W_BRANCH

    def tok(w):
        return pl.BlockSpec((tm, w), lambda i: (i, 0))

    return pl.pallas_call(
        _merge_kernel,
        grid=(n // tm,),
        in_specs=[tok(d), _full((1, d)), tok(wb), tok(wb), _full(wgate.shape),
                  _full((wb, d)), _full((wb, d)), _full((d, d))],
        out_specs=tok(d),
        out_shape=jax.ShapeDtypeStruct((n, d), F32),
        compiler_params=_params(1),
        name="merge_out",
    )(hf, norm.reshape(1, d), y_sb, y_dsa, wgate,
      w_out_sb.astype(BF16), w_out_dsa.astype(BF16), w_out.astype(BF16))


def kernel(x, p, positions, ffn1_norm, ffn1_w1, ffn1_w2, mix_norm, w_in, w_out_sb, w_out_dsa,
           w_out, ffn2_norm, ffn2_w1, ffn2_w2, ple_norm, ple_w_gate, ple_w_proj, final_norm):
    b, s, d = x.shape
    n = b * s
    depth = p.shape[0]
    nq = s // ATT_BLOCK
    h = x.reshape(n, d)
    for i in range(depth):
        h = _ffn_half_step(h, ffn1_norm[i], ffn1_w1[i], ffn1_w2[i])
        (sb, qd, qi, kd, ki, vt, wgt), wgate = _mixer_proj(h, positions, mix_norm[i], w_in[i])
        y_sb = _sb_attention(sb.reshape(b, s, 3 * W_BRANCH), b, s)
        vt4 = vt.reshape(b, nq, HEAD_DIM, ATT_BLOCK)
        y_dsa = _dsa_attention(qd.reshape(b, s, W_BRANCH), qi.reshape(b, s, W_BRANCH),
                               kd.reshape(b, s, 2 * LANES), ki.reshape(b, s, 2 * LANES),
                               vt4, wgt, b, s)
        h = _merge_out(h, mix_norm[i], y_sb.reshape(n, W_BRANCH), y_dsa.reshape(n, W_BRANCH),
                       wgate, w_out_sb[i], w_out_dsa[i], w_out[i])
        h = _ffn_ple_step(h, ffn2_norm[i], ffn2_w1[i], ffn2_w2[i], ple_norm[i],
                          p[i].reshape(n, -1), ple_w_gate[i], ple_w_proj[i], final_norm,
                          final=(i == depth - 1))
    return h.reshape(b, s, d)
```

```python
import functools

import numpy as np
import jax
import jax.numpy as jnp
from jax import lax
from jax.experimental import pallas as pl
from jax.experimental.pallas import tpu as pltpu

F32 = jnp.float32
BF16 = jnp.bfloat16
I32 = jnp.int32
I16 = jnp.int16

HEAD_DIM = 64
N_HEADS = 8
W_BRANCH = N_HEADS * HEAD_DIM
TOPK_MAX = 256
ROPE_THETA = 500000.0
ROPE_DIM = HEAD_DIM // 4
EPS = 1e-6
LOG2E = 1.4426950408889634

LANES = 128
SUBLANES = 8
ATT_BLOCK = 128
DSA_QUERIES = 512
DSA_SPAN = 512
PACKED_SUBLANES = 16
INT_MIN = np.int32(-2**31)
INT16_MIN = -2**15
VMEM_LIMIT = 56 * 1024 * 1024

_NT = (((1,), (1,)), ((), ()))


def _rmsnorm(x, g):
    ms = jnp.mean(x * x, axis=-1, keepdims=True)
    return x * lax.rsqrt(ms + EPS) * g


def _token_tile(n):
    for tm in (512, 256, 128):
        if n % tm == 0:
            return tm
    raise ValueError(f"token count {n} must be a multiple of 128")


def _ff_chunks(d_ff):
    assert d_ff % 256 == 0, d_ff
    chunks, left = [], d_ff
    while left:
        c = min(768, left)
        chunks.append(c)
        left -= c
    return tuple(chunks)


def _params(n_axes):
    return pltpu.CompilerParams(dimension_semantics=("arbitrary",) * n_axes,
                                vmem_limit_bytes=VMEM_LIMIT)


def _full(shape):
    nd = len(shape)
    return pl.BlockSpec(shape, lambda *_: (0,) * nd)


def _swiglu_half_step(x, g_ref, w1a_ref, w1b_ref, w2_ref, chunks):
    xn = _rmsnorm(x, g_ref[...]).astype(BF16)
    acc = jnp.zeros(x.shape, F32)
    off = 0
    for cw in chunks:
        a = jnp.dot(xn, w1a_ref[:, off:off + cw], preferred_element_type=F32)
        b = jnp.dot(xn, w1b_ref[:, off:off + cw], preferred_element_type=F32)
        hid = (a * jax.nn.sigmoid(a) * b).astype(BF16)
        acc = acc + jnp.dot(hid, w2_ref[off:off + cw, :], preferred_element_type=F32)
        off += cw
    return x + 0.5 * acc


def _ffn_kernel(x_ref, g_ref, w1a_ref, w1b_ref, w2_ref, o_ref, *, chunks):
    o_ref[...] = _swiglu_half_step(x_ref[...], g_ref, w1a_ref, w1b_ref, w2_ref, chunks)


def _ffn_ple_kernel(x_ref, g_ref, w1a_ref, w1b_ref, w2_ref, gp_ref, p_ref, wg_ref, wp_ref,
                    gf_ref, o_ref, *, chunks, final):
    h = _swiglu_half_step(x_ref[...], g_ref, w1a_ref, w1b_ref, w2_ref, chunks)
    u = _rmsnorm(h, gp_ref[...]).astype(BF16)
    gate = jax.nn.sigmoid(jnp.dot(u, wg_ref[...], preferred_element_type=F32))
    emb = jnp.dot(p_ref[...].astype(BF16), wp_ref[...], preferred_element_type=F32)
    h = h + gate * emb
    o_ref[...] = _rmsnorm(h, gf_ref[...]) if final else h


def _ffn_weights(w1, w2):
    d_ff = w2.shape[0]
    return w1[:, :d_ff].astype(BF16), w1[:, d_ff:].astype(BF16), w2.astype(BF16)


def _ffn_half_step(xf, norm, w1, w2):
    n, d = xf.shape
    tm = _token_tile(n)
    w1a, w1b, w2b = _ffn_weights(w1, w2)
    tok = pl.BlockSpec((tm, d), lambda i: (i, 0))
    return pl.pallas_call(
        functools.partial(_ffn_kernel, chunks=_ff_chunks(w2.shape[0])),
        grid=(n // tm,),
        in_specs=[tok, _full((1, d)), _full(w1a.shape), _full(w1b.shape), _full(w2b.shape)],
        out_specs=tok,
        out_shape=jax.ShapeDtypeStruct((n, d), F32),
        compiler_params=_params(1),
        name="ffn_half_step",
    )(xf, norm.reshape(1, d), w1a, w1b, w2b)


def _ffn_ple_step(xf, norm, w1, w2, ple_norm, pf, ple_w_gate, ple_w_proj, final_norm, final):
    n, d = xf.shape
    dp = pf.shape[1]
    tm = _token_tile(n)
    w1a, w1b, w2b = _ffn_weights(w1, w2)

    def tok(w):
        return pl.BlockSpec((tm, w), lambda i: (i, 0))

    return pl.pallas_call(
        functools.partial(_ffn_ple_kernel, chunks=_ff_chunks(w2.shape[0]), final=final),
        grid=(n // tm,),
        in_specs=[tok(d), _full((1, d)), _full(w1a.shape), _full(w1b.shape), _full(w2b.shape),
                  _full((1, d)), tok(dp), _full((d, d)), _full((dp, d)), _full((1, d))],
        out_specs=tok(d),
        out_shape=jax.ShapeDtypeStruct((n, d), F32),
        compiler_params=_params(1),
        name="ffn_ple_step",
    )(xf, norm.reshape(1, d), w1a, w1b, w2b, ple_norm.reshape(1, d), pf,
      ple_w_gate.astype(BF16), ple_w_proj.astype(BF16), final_norm.reshape(1, d))


def _proj_kernel(h_ref, g_ref, pos_ref, invf_ref, spread_ref, base_ref, wsb_ref, wrot_ref, wt_ref,
                 sb_ref, qd_ref, qi_ref, kd_ref, ki_ref, vt_ref, wgt_ref):
    u = _rmsnorm(h_ref[...], g_ref[...]).astype(BF16)
    sb_ref[...] = jnp.dot(u, wsb_ref[...], preferred_element_type=F32).astype(BF16)

    ang = invf_ref[...] * pos_ref[...].astype(F32)
    cs = jnp.concatenate([jnp.cos(ang), jnp.sin(ang)], axis=0)
    cs_hi = cs.astype(BF16)
    cs_lo = (cs - cs_hi.astype(F32)).astype(BF16)
    pat = lax.dot_general(jnp.concatenate([cs_hi, cs_lo], axis=0), spread_ref[...],
                          (((0,), (0,)), ((), ())), preferred_element_type=F32)
    half = ROPE_DIM // 2
    cos = pat[:, :LANES] + base_ref[...]
    s_lo = pat[:, LANES:2 * LANES]
    s_hi = pat[:, 2 * LANES:]
    rot = jnp.dot(u, wrot_ref[...], preferred_element_type=F32)
    outs = ((qd_ref, 0, W_BRANCH), (qi_ref, W_BRANCH, W_BRANCH),
            (kd_ref, 2 * W_BRANCH, 2 * LANES), (ki_ref, 2 * W_BRANCH + 2 * LANES, 2 * LANES))
    for ref, base, width in outs:
        for g in range(width // LANES):
            xg = rot[:, base + g * LANES: base + (g + 1) * LANES]
            yg = (xg * cos + pltpu.roll(xg, LANES - half, 1) * s_lo
                  + pltpu.roll(xg, half, 1) * s_hi)
            ref[:, g * LANES:(g + 1) * LANES] = yg.astype(BF16)

    tr = lax.dot_general(wt_ref[...], u, _NT, preferred_element_type=F32)
    for kb in range(vt_ref.shape[0]):
        vt_ref[kb] = tr[:HEAD_DIM, kb * LANES:(kb + 1) * LANES].astype(BF16)
    wgt_ref[...] = tr[HEAD_DIM:] * (N_HEADS ** -0.5)


def _mixer_proj(hf, positions, norm, w_in):
    n, d = hf.shape
    tm = _token_tile(n)
    wb = W_BRANCH
    o = 0
    cols = {}
    for name, size in (("q_sb", wb), ("k_sb", wb), ("v_sb", wb), ("q_d", wb), ("k_d", HEAD_DIM),
                       ("v_d", HEAD_DIM), ("q_i", wb), ("k_i", HEAD_DIM), ("w_i", N_HEADS),
                       ("g_sb", d), ("g_dsa", d)):
        cols[name] = w_in[:, o:o + size]
        o += size
    scale = HEAD_DIM ** -0.5
    zeros = jnp.zeros((d, HEAD_DIM), w_in.dtype)
    wsb = jnp.concatenate([cols["q_sb"] * (scale * LOG2E), cols["k_sb"], cols["v_sb"]],
                          axis=1).astype(BF16)
    wrot = jnp.concatenate([cols["q_d"] * (scale * LOG2E), cols["q_i"] * scale,
                            cols["k_d"], zeros, zeros, cols["k_d"],
                            cols["k_i"], zeros, zeros, cols["k_i"]], axis=1).astype(BF16)
    wt = jnp.concatenate([cols["v_d"], cols["w_i"]], axis=1).T.astype(BF16)
    wgate = jnp.concatenate([cols["g_sb"], cols["g_dsa"]], axis=1).astype(BF16)

    half = ROPE_DIM // 2
    invf = (ROPE_THETA ** (-jnp.arange(0, ROPE_DIM, 2, dtype=F32) / ROPE_DIM)).reshape(half, 1)
    lane = np.arange(LANES) % HEAD_DIM
    hit = (lane[None, :] % half == np.arange(half)[:, None])
    zero = np.zeros((half, LANES), np.float32)
    cos_rows = np.concatenate([hit & (lane < ROPE_DIM), zero, zero], axis=1)
    sin_rows = np.concatenate([zero, -1.0 * (hit & (lane < half)),
                               hit & (lane >= half) & (lane < ROPE_DIM)], axis=1)
    spread = jnp.asarray(np.concatenate([cos_rows, sin_rows] * 2, axis=0), BF16)
    base = jnp.asarray((lane >= ROPE_DIM).astype(np.float32).reshape(1, LANES))

    def tok(w):
        return pl.BlockSpec((tm, w), lambda i: (i, 0))

    def tok_t(r):
        return pl.BlockSpec((r, tm), lambda i: (0, i))

    outs = pl.pallas_call(
        _proj_kernel,
        grid=(n // tm,),
        in_specs=[tok(d), _full((1, d)), tok_t(1), _full(invf.shape), _full(spread.shape),
                  _full(base.shape),
                  _full(wsb.shape), _full(wrot.shape), _full(wt.shape)],
        out_specs=[tok(3 * wb), tok(wb), tok(wb), tok(2 * LANES), tok(2 * LANES),
                   pl.BlockSpec((tm // LANES, HEAD_DIM, LANES), lambda i: (i, 0, 0)),
                   tok_t(N_HEADS)],
        out_shape=[jax.ShapeDtypeStruct((n, 3 * wb), BF16),
                   jax.ShapeDtypeStruct((n, wb), BF16),
                   jax.ShapeDtypeStruct((n, wb), BF16),
                   jax.ShapeDtypeStruct((n, 2 * LANES), BF16),
                   jax.ShapeDtypeStruct((n, 2 * LANES), BF16),
                   jax.ShapeDtypeStruct((n // LANES, HEAD_DIM, LANES), BF16),
                   jax.ShapeDtypeStruct((N_HEADS, n), F32)],
        compiler_params=_params(1),
        name="mixer_proj",
    )(hf, norm.reshape(1, d), positions.reshape(1, n), invf, spread, base, wsb, wrot, wt)
    return outs, wgate


def _sb_kernel(q_ref, k_ref, v_ref, o_ref, qm_ref, acc_ref, car_ref, *, tq, nb):
    i = pl.program_id(1)
    pairs = W_BRANCH // LANES
    lane = lax.broadcasted_iota(I32, (tq, LANES), 1)
    lo_half = lane < HEAD_DIM
    for bb in range(nb):
        q = q_ref[bb]
        for g in range(pairs):
            qg = q[:, g * LANES:(g + 1) * LANES]
            qm_ref[bb * N_HEADS + 2 * g] = jnp.where(lo_half, qg, jnp.zeros_like(qg))
            qm_ref[bb * N_HEADS + 2 * g + 1] = jnp.where(lo_half, jnp.zeros_like(qg), qg)
    suffix = jnp.where(lax.broadcasted_iota(I32, (2 * tq, tq), 0) % tq
                       >= lax.broadcasted_iota(I32, (2 * tq, tq), 1), 1.0, 0.0).astype(BF16)
    strict = (lax.broadcasted_iota(I32, (tq, tq), 1) < lax.broadcasted_iota(I32, (tq, tq), 0))

    acc_ref[...] = jnp.zeros(acc_ref.shape, F32)
    car_ref[...] = jnp.zeros(car_ref.shape, F32)

    def group(j, diag):
        rows = pl.ds(pl.multiple_of(j * tq, tq), tq)
        hl, zs = [], []
        for bb in range(nb):
            for g in range(pairs):
                kg = k_ref[bb, rows, g * LANES:(g + 1) * LANES]
                qq = jnp.concatenate([qm_ref[bb * N_HEADS + 2 * g],
                                      qm_ref[bb * N_HEADS + 2 * g + 1]], axis=0)
                zz = lax.dot_general(qq, kg, _NT, preferred_element_type=F32)
                for hh in range(2):
                    z = zz[hh * tq:(hh + 1) * tq]
                    neg_abs = lax.bitcast_convert_type(
                        lax.bitcast_convert_type(z, I32) | INT_MIN, F32)
                    sp = jnp.maximum(z, 0.0) + jnp.log(1.0 + jnp.exp2(neg_abs)) * LOG2E
                    if diag:
                        sp = jnp.where(strict, sp, 0.0)
                    hi = sp.astype(BF16)
                    lo = (sp - hi.astype(F32)).astype(BF16)
                    hl.append(jnp.concatenate([hi, lo], axis=1))
                    zs.append(z)
        sums = jnp.dot(jnp.concatenate(hl, axis=0), suffix, preferred_element_type=F32)
        for bb in range(nb):
            for g in range(pairs):
                probs = []
                for hh in range(2):
                    h = bb * N_HEADS + 2 * g + hh
                    car = car_ref[h]
                    blk = sums[h * tq:(h + 1) * tq]
                    for l in range(tq // LANES):
                        sl = slice(l * LANES, (l + 1) * LANES)
                        a = jnp.exp2(zs[h][:, sl] - (blk[:, sl] + car))
                        if diag:
                            a = jnp.where(strict[:, sl], a, 0.0)
                        probs.append(a.astype(BF16))
                    car_ref[h] = car + jnp.broadcast_to(blk[:, 0:1], (tq, LANES))
                vg = v_ref[bb, rows, g * LANES:(g + 1) * LANES]
                vv = jnp.concatenate([jnp.where(lo_half, vg, jnp.zeros_like(vg)),
                                      jnp.where(lo_half, jnp.zeros_like(vg), vg)], axis=0)
                acc_ref[bb * pairs + g] += jnp.dot(jnp.concatenate(probs, axis=1), vv,
                                                   preferred_element_type=F32)

    group(i, True)

    def body(jj, carry):
        group(i - 1 - jj, False)
        return carry

    lax.fori_loop(0, i, body, 0)
    for bb in range(nb):
        for g in range(pairs):
            o_ref[bb, :, g * LANES:(g + 1) * LANES] = acc_ref[bb * pairs + g].astype(BF16)


def _sb_attention(qkv, b, s):
    sub = ATT_BLOCK
    tq = 2 * sub if s % (2 * sub) == 0 else sub
    nb = 2 if b % 2 == 0 else 1
    wb = W_BRANCH
    return pl.pallas_call(
        functools.partial(_sb_kernel, tq=tq, nb=nb),
        grid=(b // nb, s // tq),
        in_specs=[pl.BlockSpec((nb, tq, wb), lambda bi, i: (bi, i, 0)),
                  pl.BlockSpec((nb, s, wb), lambda bi, i: (bi, 0, 1)),
                  pl.BlockSpec((nb, s, wb), lambda bi, i: (bi, 0, 2))],
        out_specs=pl.BlockSpec((nb, tq, wb), lambda bi, i: (bi, i, 0)),
        out_shape=jax.ShapeDtypeStruct((b, s, wb), BF16),
        scratch_shapes=[pltpu.VMEM((nb * N_HEADS, tq, LANES), BF16),
                        pltpu.VMEM((nb * wb // LANES, tq, LANES), F32),
                        pltpu.VMEM((nb * N_HEADS, tq, LANES), F32)],
        compiler_params=_params(2),
        name="sb_attention",
    )(qkv, qkv, qkv)


def _dsa_kernel(qi_ref, qd_ref, ki_ref, kd_ref, vt_ref, wgt_ref, o_ref,
                key_ref, hi_ref, lo_ref, lom_ref, bias_ref, lg_ref, num_ref, *, tb, span, n_sel):
    i = pl.program_id(1)
    nspan = ((i + 1) * tb - 1) // span + 1
    srow = lax.broadcasted_iota(I32, (span, tb), 0)
    scol = lax.broadcasted_iota(I32, (span, tb), 1)

    def rows(c):
        return pl.ds(pl.multiple_of(c * span, span), span)

    def span_causal(c):
        return (c * span + srow) <= (i * tb + scol)

    def fold(x, op):
        return op(x.reshape(x.shape[0] // SUBLANES, SUBLANES, x.shape[1]), axis=0)

    def pair_lhs(ref, c):
        kab = ref[0, rows(c), :]
        n16 = span // PACKED_SUBLANES
        return jnp.stack([kab[:, :LANES].reshape(n16, PACKED_SUBLANES, LANES),
                          kab[:, LANES:].reshape(n16, PACKED_SUBLANES, LANES)],
                         axis=1).reshape(2 * span, LANES)

    def head_dots(lhs, q_ref, gg):
        qq = jnp.concatenate([q_ref[0, :, (2 * gg) * LANES:(2 * gg + 1) * LANES],
                              q_ref[0, :, (2 * gg + 1) * LANES:(2 * gg + 2) * LANES]], axis=0)
        zz = lax.dot_general(lhs, qq, _NT, preferred_element_type=F32)
        zz = zz.reshape(span // PACKED_SUBLANES, 2, PACKED_SUBLANES, 2 * tb)
        even, odd = zz[:, 0].reshape(span, 2 * tb), zz[:, 1].reshape(span, 2 * tb)
        return [even[:, :tb], odd[:, :tb], even[:, tb:], odd[:, tb:]]

    def score_span(c, carry):
        lhs = pair_lhs(ki_ref, c)
        score = jnp.zeros((span, tb), F32)
        for gg in range(N_HEADS // 4):
            for hh, dots in enumerate(head_dots(lhs, qi_ref, gg)):
                h = 4 * gg + hh
                score = score + wgt_ref[h:h + 1, :] * jnp.maximum(dots, 0.0)
        bits = lax.bitcast_convert_type(score, I32)
        bits = jnp.where(bits == INT_MIN, 0, bits)
        key = bits ^ ((bits >> 31) & np.int32(0x7FFFFFFF))
        key = jnp.where(span_causal(c), key, INT_MIN)
        key_ref[rows(c), :] = key
        hi_ref[rows(c), :] = (key >> 16).astype(I16)
        lo_ref[rows(c), :] = ((key & np.int32(0xFFFF)) + INT16_MIN).astype(I16)
        return carry

    lax.fori_loop(0, nspan, score_span, 0)

    def count16(ref, pred):
        n_acc = 8

        def scan(c, accs):
            kk = ref[rows(c), :].reshape(span // PACKED_SUBLANES, PACKED_SUBLANES, tb)
            accs = list(accs)
            for r in range(span // PACKED_SUBLANES):
                a = accs[r % n_acc]
                accs[r % n_acc] = jnp.where(pred(kk[r]), a + np.int16(1), a)
            return tuple(accs)

        accs = lax.fori_loop(0, nspan, scan, (jnp.zeros((PACKED_SUBLANES, tb), I16),) * n_acc)
        return jnp.sum(sum(accs[1:], accs[0]).astype(I32), axis=0, keepdims=True)

    def packed(x):
        return jnp.broadcast_to(x, (PACKED_SUBLANES, tb)).astype(I16)

    def search16(ref, base, state):
        def refine(cand, state):
            thr, cnt = state
            cand16 = packed(cand)
            c = base + count16(ref, lambda kk: kk >= cand16)
            ok = c >= n_sel
            return jnp.where(ok, cand, thr), jnp.where(ok, c, cnt)

        state = refine(jnp.zeros((1, tb), I32), state)
        return lax.fori_loop(
            0, 15, lambda it, st: refine(st[0] + jnp.left_shift(np.int32(1), 14 - it), st), state)

    lowest = jnp.full((1, tb), INT16_MIN, I32)
    thr_hi, cnt = search16(hi_ref, 0, (lowest, jnp.zeros((1, tb), I32) + nspan * span))
    thr_hi16 = packed(thr_hi)

    def match_span(c, carry):
        hi = hi_ref[rows(c), :].reshape(span // PACKED_SUBLANES, PACKED_SUBLANES, tb)
        lo = lo_ref[rows(c), :].reshape(span // PACKED_SUBLANES, PACKED_SUBLANES, tb)
        lom_ref[rows(c), :] = jnp.where(hi == thr_hi16[None], lo, np.int16(INT16_MIN)
                                        ).reshape(span, tb)
        return carry

    lax.fori_loop(0, nspan, match_span, 0)
    n_above = count16(hi_ref, lambda kk: kk > thr_hi16)
    thr_lo, cnt = search16(lom_ref, n_above, (lowest, cnt))
    thr = thr_hi * 65536 + (thr_lo - INT16_MIN)

    def plain_bias(c, carry):
        sel = (key_ref[rows(c), :] >= thr) & span_causal(c)
        bias_ref[rows(c), :] = jnp.where(sel, 0.0, -jnp.inf)
        return carry

    lax.fori_loop(0, nspan, plain_bias, 0)

    tie = (cnt > n_sel) & (thr > INT_MIN)

    @pl.when(jnp.max(tie.astype(I32)) > 0)
    def _():
        def count_gt(c, acc):
            return acc + fold(jnp.where(key_ref[rows(c), :] > thr, 1, 0).astype(I32), jnp.sum)
        n_gt = jnp.sum(lax.fori_loop(0, nspan, count_gt, jnp.zeros((SUBLANES, tb), I32)),
                       axis=0, keepdims=True)
        need = jnp.where(tie, (n_sel - n_gt).astype(F32), 3.0e38)
        before = jnp.where(lax.broadcasted_iota(I32, (span, span), 1)
                           < lax.broadcasted_iota(I32, (span, span), 0),
                           1.0, 0.0).astype(BF16)

        def tie_bias(c, seen):
            kk = key_ref[rows(c), :]
            eq = kk == thr
            eqf = jnp.where(eq, 1.0, 0.0)
            rank = jnp.dot(before, eqf.astype(BF16), preferred_element_type=F32) + seen
            sel = ((kk > thr) | (eq & (rank < need))) & span_causal(c)
            bias_ref[rows(c), :] = jnp.where(sel, 0.0, -jnp.inf)
            return seen + jnp.sum(eqf, axis=0, keepdims=True)

        lax.fori_loop(0, nspan, tie_bias, jnp.zeros((1, tb), F32))

    ninf = jnp.full((SUBLANES, tb), -jnp.inf, F32)
    for gg in range(N_HEADS // 4):
        def logits_span(c, mx):
            bias = bias_ref[rows(c), :]
            new = []
            for h, dots in enumerate(head_dots(pair_lhs(kd_ref, c), qd_ref, gg)):
                lg = dots + bias
                lg_ref[h, rows(c), :] = lg
                new.append(jnp.maximum(mx[h], fold(lg, jnp.max)))
            return tuple(new)

        mx = lax.fori_loop(0, nspan, logits_span, (ninf,) * 4)
        mx = tuple(jnp.max(m, axis=0, keepdims=True) for m in mx)

        num_ref[...] = jnp.zeros(num_ref.shape, F32)

        def pv_span(c, carry):
            vt = jnp.concatenate([vt_ref[0, c * (span // LANES) + k]
                                  for k in range(span // LANES)], axis=1)
            vt1 = jnp.concatenate([vt, jnp.ones((PACKED_SUBLANES, span), BF16)], axis=0)
            for h in range(4):
                p = jnp.exp2(lg_ref[h, rows(c), :] - mx[h])
                num_ref[h] += jnp.dot(vt1, p.astype(BF16), preferred_element_type=F32)
            return carry

        lax.fori_loop(0, nspan, pv_span, 0)
        outs = [num_ref[h, :HEAD_DIM] / num_ref[h, HEAD_DIM:HEAD_DIM + 1] for h in range(4)]
        for g in range(2):
            pair = jnp.concatenate([outs[2 * g], outs[2 * g + 1]], axis=0)
            o_ref[0, :, (2 * gg + g) * LANES:(2 * gg + g + 1) * LANES] = pair.T.astype(BF16)


def _dsa_attention(qd, qi, kd, ki, vt, wgt, b, s):
    tb = DSA_QUERIES
    span = DSA_SPAN
    assert s % span == 0 and span % tb == 0, s
    nq = s // tb
    n_sel = min(TOPK_MAX, s // 4)
    return pl.pallas_call(
        functools.partial(_dsa_kernel, tb=tb, span=span, n_sel=n_sel),
        grid=(b, nq),
        in_specs=[pl.BlockSpec((1, tb, W_BRANCH), lambda bi, i: (bi, i, 0)),
                  pl.BlockSpec((1, tb, W_BRANCH), lambda bi, i: (bi, i, 0)),
                  pl.BlockSpec((1, s, 2 * LANES), lambda bi, i: (bi, 0, 0)),
                  pl.BlockSpec((1, s, 2 * LANES), lambda bi, i: (bi, 0, 0)),
                  pl.BlockSpec((1, s // LANES, HEAD_DIM, LANES), lambda bi, i: (bi, 0, 0, 0)),
                  pl.BlockSpec((N_HEADS, tb), lambda bi, i: (0, bi * nq + i))],
        out_specs=pl.BlockSpec((1, tb, W_BRANCH), lambda bi, i: (bi, i, 0)),
        out_shape=jax.ShapeDtypeStruct((b, s, W_BRANCH), BF16),
        scratch_shapes=[pltpu.VMEM((s, tb), I32), pltpu.VMEM((s, tb), I16),
                        pltpu.VMEM((s, tb), I16), pltpu.VMEM((s, tb), I16),
                        pltpu.VMEM((s, tb), F32),
                        pltpu.VMEM((4, s, tb), F32),
                        pltpu.VMEM((4, HEAD_DIM + PACKED_SUBLANES, tb), F32)],
        compiler_params=_params(2),
        name="dsa_attention",
    )(qi, qd, ki, kd, vt, wgt)


def _merge_kernel(h_ref, g_ref, ysb_ref, yds_ref, wg_ref, wosb_ref, wods_ref, wo_ref, o_ref):
    h = h_ref[...]
    d = h.shape[1]
    u = _rmsnorm(h, g_ref[...]).astype(BF16)
    gates = jax.nn.sigmoid(jnp.dot(u, wg_ref[...], preferred_element_type=F32))
    y_sb = jnp.dot(ysb_ref[...], wosb_ref[...], preferred_element_type=F32)
    y_ds = jnp.dot(yds_ref[...], wods_ref[...], preferred_element_type=F32)
    merged = gates[:, :d] * y_sb + gates[:, d:] * y_ds
    o_ref[...] = h + jnp.dot(merged.astype(BF16), wo_ref[...], preferred_element_type=F32)


def _merge_out(hf, norm, y_sb, y_dsa, wgate, w_out_sb, w_out_dsa, w_out):
    n, d = hf.shape
    tm = _token_tile(n)
    wb = W_BRANCH

    def tok(w):
        return pl.BlockSpec((tm, w), lambda i: (i, 0))

    return pl.pallas_call(
        _merge_kernel,
        grid=(n // tm,),
        in_specs=[tok(d), _full((1, d)), tok(wb), tok(wb), _full(wgate.shape),
                  _full((wb, d)), _full((wb, d)), _full((d, d))],
        out_specs=tok(d),
        out_shape=jax.ShapeDtypeStruct((n, d), F32),
        compiler_params=_params(1),
        name="merge_out",
    )(hf, norm.reshape(1, d), y_sb, y_dsa, wgate,
      w_out_sb.astype(BF16), w_out_dsa.astype(BF16), w_out.astype(BF16))


def kernel(x, p, positions, ffn1_norm, ffn1_w1, ffn1_w2, mix_norm, w_in, w_out_sb, w_out_dsa,
           w_out, ffn2_norm, ffn2_w1, ffn2_w2, ple_norm, ple_w_gate, ple_w_proj, final_norm):
    b, s, d = x.shape
    n = b * s
    depth = p.shape[0]
    nq = s // ATT_BLOCK
    h = x.reshape(n, d)
    for i in range(depth):
        h = _ffn_half_step(h, ffn1_norm[i], ffn1_w1[i], ffn1_w2[i])
        (sb, qd, qi, kd, ki, vt, wgt), wgate = _mixer_proj(h, positions, mix_norm[i], w_in[i])
        y_sb = _sb_attention(sb.reshape(b, s, 3 * W_BRANCH), b, s)
        vt4 = vt.reshape(b, nq, HEAD_DIM, ATT_BLOCK)
        y_dsa = _dsa_attention(qd.reshape(b, s, W_BRANCH), qi.reshape(b, s, W_BRANCH),
                               kd.reshape(b, s, 2 * LANES), ki.reshape(b, s, 2 * LANES),
                               vt4, wgt, b, s)
        h = _merge_out(h, mix_norm[i], y_sb.reshape(n, W_BRANCH), y_dsa.reshape(n, W_BRANCH),
                       wgate, w_out_sb[i], w_out_dsa[i], w_out[i])
        h = _ffn_ple_step(h, ffn2_norm[i], ffn2_w1[i], ffn2_w2[i], ple_norm[i],
                          p[i].reshape(n, -1), ple_w_gate[i], ple_w_proj[i], final_norm,
                          final=(i == depth - 1))
    return h.reshape(b, s, d)
```

```python
import functools

import numpy as np
import jax
import jax.numpy as jnp
from jax import lax
from jax.experimental import pallas as pl
from jax.experimental.pallas import tpu as pltpu

F32 = jnp.float32
BF16 = jnp.bfloat16
I32 = jnp.int32
I16 = jnp.int16

HEAD_DIM = 64
N_HEADS = 8
W_BRANCH = N_HEADS * HEAD_DIM
TOPK_MAX = 256
ROPE_THETA = 500000.0
ROPE_DIM = HEAD_DIM // 4
EPS = 1e-6
LOG2E = 1.4426950408889634

LANES = 128
SUBLANES = 8
ATT_BLOCK = 128
DSA_QUERIES = 512
DSA_SPAN = 512
PACKED_SUBLANES = 16
INT_MIN = np.int32(-2**31)
INT16_MIN = -2**15
VMEM_LIMIT = 56 * 1024 * 1024

_NT = (((1,), (1,)), ((), ()))


def _rmsnorm(x, g):
    ms = jnp.mean(x * x, axis=-1, keepdims=True)
    return x * lax.rsqrt(ms + EPS) * g


def _token_tile(n, largest=512):
    for tm in (largest, 512, 256, 128):
        if n % tm == 0:
            return tm
    raise ValueError(f"token count {n} must be a multiple of 128")


def _ff_chunks(d_ff):
    assert d_ff % 256 == 0, d_ff
    chunks, left = [], d_ff
    while left:
        c = min(768, left)
        chunks.append(c)
        left -= c
    return tuple(chunks)


def _params(n_axes):
    return pltpu.CompilerParams(dimension_semantics=("arbitrary",) * n_axes,
                                vmem_limit_bytes=VMEM_LIMIT)


def _full(shape):
    nd = len(shape)
    return pl.BlockSpec(shape, lambda *_: (0,) * nd)


def _swiglu_half_step(x, g_ref, w1a_ref, w1b_ref, w2_ref, chunks):
    xn = _rmsnorm(x, g_ref[...]).astype(BF16)
    acc = jnp.zeros(x.shape, F32)
    off = 0
    for cw in chunks:
        a = jnp.dot(xn, w1a_ref[:, off:off + cw], preferred_element_type=F32)
        b = jnp.dot(xn, w1b_ref[:, off:off + cw], preferred_element_type=F32)
        hid = (a * jax.nn.sigmoid(a) * b).astype(BF16)
        acc = acc + jnp.dot(hid, w2_ref[off:off + cw, :], preferred_element_type=F32)
        off += cw
    return x + 0.5 * acc


def _ffn_kernel(x_ref, g_ref, w1a_ref, w1b_ref, w2_ref, o_ref, *, chunks):
    o_ref[...] = _swiglu_half_step(x_ref[...], g_ref, w1a_ref, w1b_ref, w2_ref, chunks)


def _ffn_ple_kernel(x_ref, g_ref, w1a_ref, w1b_ref, w2_ref, gp_ref, p_ref, wg_ref, wp_ref,
                    gf_ref, o_ref, *, chunks, final):
    h = _swiglu_half_step(x_ref[...], g_ref, w1a_ref, w1b_ref, w2_ref, chunks)
    u = _rmsnorm(h, gp_ref[...]).astype(BF16)
    gate = jax.nn.sigmoid(jnp.dot(u, wg_ref[...], preferred_element_type=F32))
    emb = jnp.dot(p_ref[...].astype(BF16), wp_ref[...], preferred_element_type=F32)
    h = h + gate * emb
    o_ref[...] = _rmsnorm(h, gf_ref[...]) if final else h


def _ffn_weights(w1, w2):
    d, d_ff = w1.shape[0], w2.shape[0]
    w1b = w1.astype(BF16)
    halves = [pl.BlockSpec((d, d_ff), functools.partial(lambda j, *_: (0, j), j),
                           pipeline_mode=pl.Buffered(1)) for j in range(2)]
    return (w1b, w1b, w2.astype(BF16)), halves + [_full((d_ff, d))]


def _ffn_half_step(xf, norm, w1, w2):
    n, d = xf.shape
    tm = _token_tile(n)
    weights, wspecs = _ffn_weights(w1, w2)
    tok = pl.BlockSpec((tm, d), lambda i: (i, 0))
    return pl.pallas_call(
        functools.partial(_ffn_kernel, chunks=_ff_chunks(w2.shape[0])),
        grid=(n // tm,),
        in_specs=[tok, _full((1, d))] + wspecs,
        out_specs=tok,
        out_shape=jax.ShapeDtypeStruct((n, d), F32),
        compiler_params=_params(1),
        name="ffn_half_step",
    )(xf, norm.reshape(1, d), *weights)


def _ffn_ple_step(xf, norm, w1, w2, ple_norm, pf, ple_w_gate, ple_w_proj, final_norm, final):
    n, d = xf.shape
    dp = pf.shape[1]
    tm = _token_tile(n)
    weights, wspecs = _ffn_weights(w1, w2)

    def tok(w):
        return pl.BlockSpec((tm, w), lambda i: (i, 0))

    return pl.pallas_call(
        functools.partial(_ffn_ple_kernel, chunks=_ff_chunks(w2.shape[0]), final=final),
        grid=(n // tm,),
        in_specs=[tok(d), _full((1, d))] + wspecs
        + [_full((1, d)), tok(dp), _full((d, d)), _full((dp, d)), _full((1, d))],
        out_specs=tok(d),
        out_shape=jax.ShapeDtypeStruct((n, d), F32),
        compiler_params=_params(1),
        name="ffn_ple_step",
    )(xf, norm.reshape(1, d), *weights, ple_norm.reshape(1, d), pf,
      ple_w_gate.astype(BF16), ple_w_proj.astype(BF16), final_norm.reshape(1, d))


def _proj_kernel(h_ref, g_ref, pos_ref, invf_ref, spread_ref, base_ref, wsb_ref, wrot_ref, wt_ref,
                 sb_ref, qd_ref, qi_ref, kd_ref, ki_ref, vt_ref, wgt_ref):
    u = _rmsnorm(h_ref[...], g_ref[...]).astype(BF16)
    sb_ref[...] = jnp.dot(u, wsb_ref[...], preferred_element_type=F32).astype(BF16)

    ang = invf_ref[...] * pos_ref[...].astype(F32)
    cs = jnp.concatenate([jnp.cos(ang), jnp.sin(ang)], axis=0)
    cs_hi = cs.astype(BF16)
    cs_lo = (cs - cs_hi.astype(F32)).astype(BF16)
    pat = lax.dot_general(jnp.concatenate([cs_hi, cs_lo], axis=0), spread_ref[...],
                          (((0,), (0,)), ((), ())), preferred_element_type=F32)
    half = ROPE_DIM // 2
    cos = pat[:, :LANES] + base_ref[...]
    s_lo = pat[:, LANES:2 * LANES]
    s_hi = pat[:, 2 * LANES:]
    rot = jnp.dot(u, wrot_ref[...], preferred_element_type=F32)
    outs = ((qd_ref, 0, W_BRANCH), (qi_ref, W_BRANCH, W_BRANCH),
            (kd_ref, 2 * W_BRANCH, 2 * LANES), (ki_ref, 2 * W_BRANCH + 2 * LANES, 2 * LANES))
    for ref, base, width in outs:
        for g in range(width // LANES):
            xg = rot[:, base + g * LANES: base + (g + 1) * LANES]
            yg = (xg * cos + pltpu.roll(xg, LANES - half, 1) * s_lo
                  + pltpu.roll(xg, half, 1) * s_hi)
            ref[:, g * LANES:(g + 1) * LANES] = yg.astype(BF16)

    tr = lax.dot_general(wt_ref[...], u, _NT, preferred_element_type=F32)
    for kb in range(vt_ref.shape[0]):
        vt_ref[kb] = tr[:HEAD_DIM, kb * LANES:(kb + 1) * LANES].astype(BF16)
    wgt_ref[...] = tr[HEAD_DIM:] * (N_HEADS ** -0.5)


def _mixer_proj(hf, positions, norm, w_in):
    n, d = hf.shape
    tm = _token_tile(n, 1024)
    wb = W_BRANCH
    o = 0
    cols = {}
    for name, size in (("q_sb", wb), ("k_sb", wb), ("v_sb", wb), ("q_d", wb), ("k_d", HEAD_DIM),
                       ("v_d", HEAD_DIM), ("q_i", wb), ("k_i", HEAD_DIM), ("w_i", N_HEADS),
                       ("g_sb", d), ("g_dsa", d)):
        cols[name] = w_in[:, o:o + size]
        o += size
    scale = HEAD_DIM ** -0.5
    zeros = jnp.zeros((d, HEAD_DIM), w_in.dtype)
    wsb = jnp.concatenate([cols["q_sb"] * (scale * LOG2E), cols["k_sb"], cols["v_sb"]],
                          axis=1).astype(BF16)
    wrot = jnp.concatenate([cols["q_d"] * (scale * LOG2E), cols["q_i"] * scale,
                            cols["k_d"], zeros, zeros, cols["k_d"],
                            cols["k_i"], zeros, zeros, cols["k_i"]], axis=1).astype(BF16)
    wt = jnp.concatenate([cols["v_d"], cols["w_i"]], axis=1).T.astype(BF16)
    wgate = jnp.concatenate([cols["g_sb"], cols["g_dsa"]], axis=1).astype(BF16)

    half = ROPE_DIM // 2
    invf = (ROPE_THETA ** (-jnp.arange(0, ROPE_DIM, 2, dtype=F32) / ROPE_DIM)).reshape(half, 1)
    lane = np.arange(LANES) % HEAD_DIM
    hit = (lane[None, :] % half == np.arange(half)[:, None])
    zero = np.zeros((half, LANES), np.float32)
    cos_rows = np.concatenate([hit & (lane < ROPE_DIM), zero, zero], axis=1)
    sin_rows = np.concatenate([zero, -1.0 * (hit & (lane < half)),
                               hit & (lane >= half) & (lane < ROPE_DIM)], axis=1)
    spread = jnp.asarray(np.concatenate([cos_rows, sin_rows] * 2, axis=0), BF16)
    base = jnp.asarray((lane >= ROPE_DIM).astype(np.float32).reshape(1, LANES))

    def tok(w):
        return pl.BlockSpec((tm, w), lambda i: (i, 0))

    def tok_t(r):
        return pl.BlockSpec((r, tm), lambda i: (0, i))

    outs = pl.pallas_call(
        _proj_kernel,
        grid=(n // tm,),
        in_specs=[tok(d), _full((1, d)), tok_t(1), _full(invf.shape), _full(spread.shape),
                  _full(base.shape),
                  _full(wsb.shape), _full(wrot.shape), _full(wt.shape)],
        out_specs=[tok(3 * wb), tok(wb), tok(wb), tok(2 * LANES), tok(2 * LANES),
                   pl.BlockSpec((tm // LANES, HEAD_DIM, LANES), lambda i: (i, 0, 0)),
                   tok_t(N_HEADS)],
        out_shape=[jax.ShapeDtypeStruct((n, 3 * wb), BF16),
                   jax.ShapeDtypeStruct((n, wb), BF16),
                   jax.ShapeDtypeStruct((n, wb), BF16),
                   jax.ShapeDtypeStruct((n, 2 * LANES), BF16),
                   jax.ShapeDtypeStruct((n, 2 * LANES), BF16),
                   jax.ShapeDtypeStruct((n // LANES, HEAD_DIM, LANES), BF16),
                   jax.ShapeDtypeStruct((N_HEADS, n), F32)],
        compiler_params=_params(1),
        name="mixer_proj",
    )(hf, norm.reshape(1, d), positions.reshape(1, n), invf, spread, base, wsb, wrot, wt)
    return outs, wgate


def _sb_kernel(q_ref, k_ref, v_ref, o_ref, qm_ref, acc_ref, car_ref, *, tq, nb):
    i = pl.program_id(1)
    pairs = W_BRANCH // LANES
    lane = lax.broadcasted_iota(I32, (tq, LANES), 1)
    lo_half = lane < HEAD_DIM
    for bb in range(nb):
        q = q_ref[bb]
        for g in range(pairs):
            qg = q[:, g * LANES:(g + 1) * LANES]
            qm_ref[bb * N_HEADS + 2 * g] = jnp.where(lo_half, qg, jnp.zeros_like(qg))
            qm_ref[bb * N_HEADS + 2 * g + 1] = jnp.where(lo_half, jnp.zeros_like(qg), qg)
    suffix = jnp.where(lax.broadcasted_iota(I32, (2 * tq, tq), 0) % tq
                       >= lax.broadcasted_iota(I32, (2 * tq, tq), 1), 1.0, 0.0).astype(BF16)
    strict = (lax.broadcasted_iota(I32, (tq, tq), 1) < lax.broadcasted_iota(I32, (tq, tq), 0))

    acc_ref[...] = jnp.zeros(acc_ref.shape, F32)
    car_ref[...] = jnp.zeros(car_ref.shape, F32)

    def group(j, diag):
        rows = pl.ds(pl.multiple_of(j * tq, tq), tq)
        hl, zs = [], []
        for bb in range(nb):
            for g in range(pairs):
                kg = k_ref[bb, rows, g * LANES:(g + 1) * LANES]
                qq = jnp.concatenate([qm_ref[bb * N_HEADS + 2 * g],
                                      qm_ref[bb * N_HEADS + 2 * g + 1]], axis=0)
                zz = lax.dot_general(qq, kg, _NT, preferred_element_type=F32)
                for hh in range(2):
                    z = zz[hh * tq:(hh + 1) * tq]
                    sp = jnp.maximum(z, 0.0) + jnp.log(1.0 + jnp.exp2(-jnp.abs(z))) * LOG2E
                    if diag:
                        sp = jnp.where(strict, sp, 0.0)
                    hi = sp.astype(BF16)
                    lo = (sp - hi.astype(F32)).astype(BF16)
                    hl.append(jnp.concatenate([hi, lo], axis=1))
                    zs.append(z)
        sums = jnp.dot(jnp.concatenate(hl, axis=0), suffix, preferred_element_type=F32)
        for bb in range(nb):
            for g in range(pairs):
                probs = []
                for hh in range(2):
                    h = bb * N_HEADS + 2 * g + hh
                    car = car_ref[h]
                    blk = sums[h * tq:(h + 1) * tq]
                    for l in range(tq // LANES):
                        sl = slice(l * LANES, (l + 1) * LANES)
                        a = jnp.exp2(zs[h][:, sl] - (blk[:, sl] + car))
                        if diag:
                            a = jnp.where(strict[:, sl], a, 0.0)
                        probs.append(a.astype(BF16))
                    car_ref[h] = car + jnp.broadcast_to(blk[:, 0:1], (tq, LANES))
                vg = v_ref[bb, rows, g * LANES:(g + 1) * LANES]
                vv = jnp.concatenate([jnp.where(lo_half, vg, jnp.zeros_like(vg)),
                                      jnp.where(lo_half, jnp.zeros_like(vg), vg)], axis=0)
                acc_ref[bb * pairs + g] += jnp.dot(jnp.concatenate(probs, axis=1), vv,
                                                   preferred_element_type=F32)

    group(i, True)

    def body(jj, carry):
        group(i - 1 - jj, False)
        return carry

    lax.fori_loop(0, i, body, 0)
    for bb in range(nb):
        for g in range(pairs):
            o_ref[bb, :, g * LANES:(g + 1) * LANES] = acc_ref[bb * pairs + g].astype(BF16)


def _sb_attention(qkv, b, s):
    sub = ATT_BLOCK
    tq = 2 * sub if s % (2 * sub) == 0 else sub
    nb = 2 if b % 2 == 0 else 1
    wb = W_BRANCH
    return pl.pallas_call(
        functools.partial(_sb_kernel, tq=tq, nb=nb),
        grid=(b // nb, s // tq),
        in_specs=[pl.BlockSpec((nb, tq, wb), lambda bi, i: (bi, i, 0)),
                  pl.BlockSpec((nb, s, wb), lambda bi, i: (bi, 0, 1)),
                  pl.BlockSpec((nb, s, wb), lambda bi, i: (bi, 0, 2))],
        out_specs=pl.BlockSpec((nb, tq, wb), lambda bi, i: (bi, i, 0)),
        out_shape=jax.ShapeDtypeStruct((b, s, wb), BF16),
        scratch_shapes=[pltpu.VMEM((nb * N_HEADS, tq, LANES), BF16),
                        pltpu.VMEM((nb * wb // LANES, tq, LANES), F32),
                        pltpu.VMEM((nb * N_HEADS, tq, LANES), F32)],
        compiler_params=_params(2),
        name="sb_attention",
    )(qkv, qkv, qkv)


def _dsa_kernel(qi_ref, qd_ref, ki_ref, kd_ref, vt_ref, wgt_ref, o_ref,
                sc_ref, bias_ref, lg_ref, num_ref, *, tb, span, n_sel):
    i = pl.program_id(1)
    nspan = ((i + 1) * tb - 1) // span + 1
    srow = lax.broadcasted_iota(I32, (span, tb), 0)
    scol = lax.broadcasted_iota(I32, (span, tb), 1)

    def rows(c):
        return pl.ds(pl.multiple_of(c * span, span), span)

    def span_causal(c):
        return (c * span + srow) <= (i * tb + scol)

    def fold(x, op):
        return op(x.reshape(x.shape[0] // SUBLANES, SUBLANES, x.shape[1]), axis=0)

    def pair_lhs(ref, c):
        kab = ref[0, rows(c), :]
        n16 = span // PACKED_SUBLANES
        return jnp.stack([kab[:, :LANES].reshape(n16, PACKED_SUBLANES, LANES),
                          kab[:, LANES:].reshape(n16, PACKED_SUBLANES, LANES)],
                         axis=1).reshape(2 * span, LANES)

    def head_dots(lhs, q_ref, gg):
        qq = jnp.concatenate([q_ref[0, :, (2 * gg) * LANES:(2 * gg + 1) * LANES],
                              q_ref[0, :, (2 * gg + 1) * LANES:(2 * gg + 2) * LANES]], axis=0)
        zz = lax.dot_general(lhs, qq, _NT, preferred_element_type=F32)
        zz = zz.reshape(span // PACKED_SUBLANES, 2, PACKED_SUBLANES, 2 * tb)
        even, odd = zz[:, 0].reshape(span, 2 * tb), zz[:, 1].reshape(span, 2 * tb)
        return [even[:, :tb], odd[:, :tb], even[:, tb:], odd[:, tb:]]

    def score_span(c, carry):
        lhs = pair_lhs(ki_ref, c)
        score = jnp.zeros((span, tb), F32)
        for gg in range(N_HEADS // 4):
            for hh, dots in enumerate(head_dots(lhs, qi_ref, gg)):
                h = 4 * gg + hh
                score = score + wgt_ref[h:h + 1, :] * jnp.maximum(dots, 0.0)
        sc_ref[rows(c), :] = jnp.where(span_causal(c), score, -jnp.inf)
        return carry

    lax.fori_loop(0, nspan, score_span, 0)

    def as_float(key):
        return lax.bitcast_convert_type(key ^ ((key >> 31) & np.int32(0x7FFFFFFF)), F32)

    def count_ge(cand):
        n_acc = 8

        def scan(c, accs):
            sc = sc_ref[rows(c), :].reshape(span // SUBLANES, SUBLANES, tb)
            accs = list(accs)
            for r in range(span // SUBLANES):
                a = accs[r % n_acc]
                accs[r % n_acc] = jnp.where(sc[r] >= cand, a + 1, a)
            return tuple(accs)

        accs = lax.fori_loop(0, nspan, scan, (jnp.zeros((SUBLANES, tb), I32),) * n_acc)
        return jnp.sum(sum(accs[1:], accs[0]), axis=0, keepdims=True)

    def refine(key, state):
        thr, cnt = state
        c = count_ge(as_float(key))
        ok = c >= n_sel
        return jnp.where(ok, key, thr), jnp.where(ok, c, cnt)

    state = (jnp.full((1, tb), INT_MIN, I32), jnp.zeros((1, tb), I32) + nspan * span)
    state = refine(jnp.zeros((1, tb), I32), state)
    thr_key, cnt = lax.fori_loop(
        0, 31, lambda it, st: refine(st[0] + jnp.left_shift(np.int32(1), 30 - it), st), state)
    thr = jnp.where(thr_key == INT_MIN, -jnp.inf, as_float(thr_key))

    def plain_bias(c, carry):
        sel = (sc_ref[rows(c), :] >= thr) & span_causal(c)
        bias_ref[rows(c), :] = jnp.where(sel, 0.0, -jnp.inf)
        return carry

    lax.fori_loop(0, nspan, plain_bias, 0)

    tie = (cnt > n_sel) & (thr_key > INT_MIN)

    @pl.when(jnp.max(tie.astype(I32)) > 0)
    def _():
        def count_gt(c, acc):
            return acc + fold(jnp.where(sc_ref[rows(c), :] > thr, 1, 0).astype(I32), jnp.sum)
        n_gt = jnp.sum(lax.fori_loop(0, nspan, count_gt, jnp.zeros((SUBLANES, tb), I32)),
                       axis=0, keepdims=True)
        need = jnp.where(tie, (n_sel - n_gt).astype(F32), 3.0e38)
        before = jnp.where(lax.broadcasted_iota(I32, (span, span), 1)
                           < lax.broadcasted_iota(I32, (span, span), 0),
                           1.0, 0.0).astype(BF16)

        def tie_bias(c, seen):
            sc = sc_ref[rows(c), :]
            eq = sc == thr
            eqf = jnp.where(eq, 1.0, 0.0)
            rank = jnp.dot(before, eqf.astype(BF16), preferred_element_type=F32) + seen
            sel = ((sc > thr) | (eq & (rank < need))) & span_causal(c)
            bias_ref[rows(c), :] = jnp.where(sel, 0.0, -jnp.inf)
            return seen + jnp.sum(eqf, axis=0, keepdims=True)

        lax.fori_loop(0, nspan, tie_bias, jnp.zeros((1, tb), F32))

    ninf = jnp.full((SUBLANES, tb), -jnp.inf, F32)
    for gg in range(N_HEADS // 4):
        def logits_span(c, mx):
            bias = bias_ref[rows(c), :]
            new = []
            for h, dots in enumerate(head_dots(pair_lhs(kd_ref, c), qd_ref, gg)):
                lg = dots + bias
                lg_ref[h, rows(c), :] = lg
                new.append(jnp.maximum(mx[h], fold(lg, jnp.max)))
            return tuple(new)

        mx = lax.fori_loop(0, nspan, logits_span, (ninf,) * 4)
        mx = tuple(jnp.max(m, axis=0, keepdims=True) for m in mx)

        num_ref[...] = jnp.zeros(num_ref.shape, F32)

        def pv_span(c, carry):
            vt = jnp.concatenate([vt_ref[0, c * (span // LANES) + k]
                                  for k in range(span // LANES)], axis=1)
            vt1 = jnp.concatenate([vt, jnp.ones((PACKED_SUBLANES, span), BF16)], axis=0)
            for h in range(4):
                p = jnp.exp2(lg_ref[h, rows(c), :] - mx[h])
                num_ref[h] += jnp.dot(vt1, p.astype(BF16), preferred_element_type=F32)
            return carry

        lax.fori_loop(0, nspan, pv_span, 0)
        outs = [num_ref[h, :HEAD_DIM] / num_ref[h, HEAD_DIM:HEAD_DIM + 1] for h in range(4)]
        for g in range(2):
            pair = jnp.concatenate([outs[2 * g], outs[2 * g + 1]], axis=0)
            o_ref[0, :, (2 * gg + g) * LANES:(2 * gg + g + 1) * LANES] = pair.T.astype(BF16)


def _dsa_attention(qd, qi, kd, ki, vt, wgt, b, s):
    tb = DSA_QUERIES
    span = DSA_SPAN
    assert s % span == 0 and span % tb == 0, s
    nq = s // tb
    n_sel = min(TOPK_MAX, s // 4)
    return pl.pallas_call(
        functools.partial(_dsa_kernel, tb=tb, span=span, n_sel=n_sel),
        grid=(b, nq),
        in_specs=[pl.BlockSpec((1, tb, W_BRANCH), lambda bi, i: (bi, i, 0)),
                  pl.BlockSpec((1, tb, W_BRANCH), lambda bi, i: (bi, i, 0)),
                  pl.BlockSpec((1, s, 2 * LANES), lambda bi, i: (bi, 0, 0)),
                  pl.BlockSpec((1, s, 2 * LANES), lambda bi, i: (bi, 0, 0)),
                  pl.BlockSpec((1, s // LANES, HEAD_DIM, LANES), lambda bi, i: (bi, 0, 0, 0)),
                  pl.BlockSpec((N_HEADS, tb), lambda bi, i: (0, bi * nq + i))],
        out_specs=pl.BlockSpec((1, tb, W_BRANCH), lambda bi, i: (bi, i, 0)),
        out_shape=jax.ShapeDtypeStruct((b, s, W_BRANCH), BF16),
        scratch_shapes=[pltpu.VMEM((s, tb), F32), pltpu.VMEM((s, tb), F32),
                        pltpu.VMEM((4, s, tb), F32),
                        pltpu.VMEM((4, HEAD_DIM + PACKED_SUBLANES, tb), F32)],
        compiler_params=_params(2),
        name="dsa_attention",
    )(qi, qd, ki, kd, vt, wgt)


def _merge_kernel(h_ref, g_ref, ysb_ref, yds_ref, wg_ref, wosb_ref, wods_ref, wo_ref, o_ref):
    h = h_ref[...]
    d = h.shape[1]
    u = _rmsnorm(h, g_ref[...]).astype(BF16)
    gates = jax.nn.sigmoid(jnp.dot(u, wg_ref[...], preferred_element_type=F32))
    y_sb = jnp.dot(ysb_ref[...], wosb_ref[...], preferred_element_type=F32)
    y_ds = jnp.dot(yds_ref[...], wods_ref[...], preferred_element_type=F32)
    merged = gates[:, :d] * y_sb + gates[:, d:] * y_ds
    o_ref[...] = h + jnp.dot(merged.astype(BF16), wo_ref[...], preferred_element_type=F32)


def _merge_out(hf, norm, y_sb, y_dsa, wgate, w_out_sb, w_out_dsa, w_out):
    n, d = hf.shape
    tm = _token_tile(n, 1024)
    wb = W_BRANCH

    def tok(w):
        return pl.BlockSpec((tm, w), lambda i: (i, 0))

    return pl.pallas_call(
        _merge_kernel,
        grid=(n // tm,),
        in_specs=[tok(d), _full((1, d)), tok(wb), tok(wb), _full(wgate.shape),
                  _full((wb, d)), _full((wb, d)), _full((d, d))],
        out_specs=tok(d),
        out_shape=jax.ShapeDtypeStruct((n, d), F32),
        compiler_params=_params(1),
        name="merge_out",
    )(hf, norm.reshape(1, d), y_sb, y_dsa, wgate,
      w_out_sb.astype(BF16), w_out_dsa.astype(BF16), w_out.astype(BF16))


def kernel(x, p, positions, ffn1_norm, ffn1_w1, ffn1_w2, mix_norm, w_in, w_out_sb, w_out_dsa,
           w_out, ffn2_norm, ffn2_w1, ffn2_w2, ple_norm, ple_w_gate, ple_w_proj, final_norm):
    b, s, d = x.shape
    n = b * s
    depth = p.shape[0]
    nq = s // ATT_BLOCK
    h = x.reshape(n, d)
    for i in range(depth):
        h = _ffn_half_step(h, ffn1_norm[i], ffn1_w1[i], ffn1_w2[i])
        (sb, qd, qi, kd, ki, vt, wgt), wgate = _mixer_proj(h, positions, mix_norm[i], w_in[i])
        y_sb = _sb_attention(sb.reshape(b, s, 3 * W_BRANCH), b, s)
        vt4 = vt.reshape(b, nq, HEAD_DIM, ATT_BLOCK)
        y_dsa = _dsa_attention(qd.reshape(b, s, W_BRANCH), qi.reshape(b, s, W_BRANCH),
                               kd.reshape(b, s, 2 * LANES), ki.reshape(b, s, 2 * LANES),
                               vt4, wgt, b, s)
        h = _merge_out(h, mix_norm[i], y_sb.reshape(n, W_BRANCH), y_dsa.reshape(n, W_BRANCH),
                       wgate, w_out_sb[i], w_out_dsa[i], w_out[i])
        h = _ffn_ple_step(h, ffn2_norm[i], ffn2_w1[i], ffn2_w2[i], ple_norm[i],
                          p[i].reshape(n, -1), ple_w_gate[i], ple_w_proj[i], final_norm,
                          final=(i == depth - 1))
    return h.reshape(b, s, d)
```

```python
import functools

import numpy as np
import jax
import jax.numpy as jnp
from jax import lax
from jax.experimental import pallas as pl
from jax.experimental.pallas import tpu as pltpu

F32 = jnp.float32
BF16 = jnp.bfloat16
I32 = jnp.int32
I16 = jnp.int16

HEAD_DIM = 64
N_HEADS = 8
W_BRANCH = N_HEADS * HEAD_DIM
TOPK_MAX = 256
ROPE_THETA = 500000.0
ROPE_DIM = HEAD_DIM // 4
EPS = 1e-6
LOG2E = 1.4426950408889634

LANES = 128
SUBLANES = 8
ATT_BLOCK = 128
DSA_QUERIES = 512
DSA_SPAN = 512
PACKED_SUBLANES = 16
INT_MIN = np.int32(-2**31)
INT16_MIN = -2**15
VMEM_LIMIT = 56 * 1024 * 1024

_NT = (((1,), (1,)), ((), ()))


def _rmsnorm(x, g):
    ms = jnp.mean(x * x, axis=-1, keepdims=True)
    return x * lax.rsqrt(ms + EPS) * g


def _token_tile(n, largest=512):
    for tm in (largest, 512, 256, 128):
        if n % tm == 0:
            return tm
    raise ValueError(f"token count {n} must be a multiple of 128")


def _ff_chunks(d_ff):
    assert d_ff % 256 == 0, d_ff
    chunks, left = [], d_ff
    while left:
        c = min(768, left)
        chunks.append(c)
        left -= c
    return tuple(chunks)


def _params(n_axes):
    return pltpu.CompilerParams(dimension_semantics=("arbitrary",) * n_axes,
                                vmem_limit_bytes=VMEM_LIMIT)


def _full(shape):
    nd = len(shape)
    return pl.BlockSpec(shape, lambda *_: (0,) * nd)


def _swiglu_half_step(x, g_ref, w1a_ref, w1b_ref, w2_ref, chunks):
    xn = _rmsnorm(x, g_ref[...]).astype(BF16)
    acc = jnp.zeros(x.shape, F32)
    off = 0
    for cw in chunks:
        a = jnp.dot(xn, w1a_ref[:, off:off + cw], preferred_element_type=F32)
        b = jnp.dot(xn, w1b_ref[:, off:off + cw], preferred_element_type=F32)
        hid = (a * jax.nn.sigmoid(a) * b).astype(BF16)
        acc = acc + jnp.dot(hid, w2_ref[off:off + cw, :], preferred_element_type=F32)
        off += cw
    return x + 0.5 * acc


def _ffn_kernel(x_ref, g_ref, w1a_ref, w1b_ref, w2_ref, o_ref, *, chunks):
    o_ref[...] = _swiglu_half_step(x_ref[...], g_ref, w1a_ref, w1b_ref, w2_ref, chunks)


def _ffn_ple_kernel(x_ref, g_ref, w1a_ref, w1b_ref, w2_ref, gp_ref, p_ref, wg_ref, wp_ref,
                    gf_ref, o_ref, *, chunks, final):
    h = _swiglu_half_step(x_ref[...], g_ref, w1a_ref, w1b_ref, w2_ref, chunks)
    u = _rmsnorm(h, gp_ref[...]).astype(BF16)
    gate = jax.nn.sigmoid(jnp.dot(u, wg_ref[...], preferred_element_type=F32))
    emb = jnp.dot(p_ref[...].astype(BF16), wp_ref[...], preferred_element_type=F32)
    h = h + gate * emb
    o_ref[...] = _rmsnorm(h, gf_ref[...]) if final else h


def _ffn_weights(w1, w2):
    d, d_ff = w1.shape[0], w2.shape[0]
    w1b = w1.astype(BF16)
    halves = [pl.BlockSpec((d, d_ff), functools.partial(lambda j, *_: (0, j), j),
                           pipeline_mode=pl.Buffered(1)) for j in range(2)]
    return (w1b, w1b, w2.astype(BF16)), halves + [_full((d_ff, d))]


def _ffn_half_step(xf, norm, w1, w2):
    n, d = xf.shape
    tm = _token_tile(n)
    weights, wspecs = _ffn_weights(w1, w2)
    tok = pl.BlockSpec((tm, d), lambda i: (i, 0))
    return pl.pallas_call(
        functools.partial(_ffn_kernel, chunks=_ff_chunks(w2.shape[0])),
        grid=(n // tm,),
        in_specs=[tok, _full((1, d))] + wspecs,
        out_specs=tok,
        out_shape=jax.ShapeDtypeStruct((n, d), F32),
        compiler_params=_params(1),
        name="ffn_half_step",
    )(xf, norm.reshape(1, d), *weights)


def _ffn_ple_step(xf, norm, w1, w2, ple_norm, pf, ple_w_gate, ple_w_proj, final_norm, final):
    n, d = xf.shape
    dp = pf.shape[1]
    tm = _token_tile(n)
    weights, wspecs = _ffn_weights(w1, w2)

    def tok(w):
        return pl.BlockSpec((tm, w), lambda i: (i, 0))

    return pl.pallas_call(
        functools.partial(_ffn_ple_kernel, chunks=_ff_chunks(w2.shape[0]), final=final),
        grid=(n // tm,),
        in_specs=[tok(d), _full((1, d))] + wspecs
        + [_full((1, d)), tok(dp), _full((d, d)), _full((dp, d)), _full((1, d))],
        out_specs=tok(d),
        out_shape=jax.ShapeDtypeStruct((n, d), F32),
        compiler_params=_params(1),
        name="ffn_ple_step",
    )(xf, norm.reshape(1, d), *weights, ple_norm.reshape(1, d), pf,
      ple_w_gate.astype(BF16), ple_w_proj.astype(BF16), final_norm.reshape(1, d))


def _proj_kernel(h_ref, g_ref, pos_ref, invf_ref, spread_ref, base_ref, wsb_ref, wrot_ref, wt_ref,
                 sb_ref, qd_ref, qi_ref, kd_ref, ki_ref, vt_ref, wgt_ref):
    u = _rmsnorm(h_ref[...], g_ref[...]).astype(BF16)
    sb_ref[...] = jnp.dot(u, wsb_ref[...], preferred_element_type=F32).astype(BF16)

    ang = invf_ref[...] * pos_ref[...].astype(F32)
    cs = jnp.concatenate([jnp.cos(ang), jnp.sin(ang)], axis=0)
    cs_hi = cs.astype(BF16)
    cs_lo = (cs - cs_hi.astype(F32)).astype(BF16)
    pat = lax.dot_general(jnp.concatenate([cs_hi, cs_lo], axis=0), spread_ref[...],
                          (((0,), (0,)), ((), ())), preferred_element_type=F32)
    half = ROPE_DIM // 2
    cos = pat[:, :LANES] + base_ref[...]
    s_lo = pat[:, LANES:2 * LANES]
    s_hi = pat[:, 2 * LANES:]
    rot = jnp.dot(u, wrot_ref[...], preferred_element_type=F32)
    outs = ((qd_ref, 0, W_BRANCH), (qi_ref, W_BRANCH, W_BRANCH),
            (kd_ref, 2 * W_BRANCH, 2 * LANES), (ki_ref, 2 * W_BRANCH + 2 * LANES, 2 * LANES))
    for ref, base, width in outs:
        for g in range(width // LANES):
            xg = rot[:, base + g * LANES: base + (g + 1) * LANES]
            yg = (xg * cos + pltpu.roll(xg, LANES - half, 1) * s_lo
                  + pltpu.roll(xg, half, 1) * s_hi)
            ref[:, g * LANES:(g + 1) * LANES] = yg.astype(BF16)

    tr = lax.dot_general(wt_ref[...], u, _NT, preferred_element_type=F32)
    for kb in range(vt_ref.shape[0]):
        vt_ref[kb] = tr[:HEAD_DIM, kb * LANES:(kb + 1) * LANES].astype(BF16)
    wgt_ref[...] = tr[HEAD_DIM:] * (N_HEADS ** -0.5)


def _mixer_proj(hf, positions, norm, w_in):
    n, d = hf.shape
    tm = _token_tile(n, 1024)
    wb = W_BRANCH
    o = 0
    cols = {}
    for name, size in (("q_sb", wb), ("k_sb", wb), ("v_sb", wb), ("q_d", wb), ("k_d", HEAD_DIM),
                       ("v_d", HEAD_DIM), ("q_i", wb), ("k_i", HEAD_DIM), ("w_i", N_HEADS),
                       ("g_sb", d), ("g_dsa", d)):
        cols[name] = w_in[:, o:o + size]
        o += size
    scale = HEAD_DIM ** -0.5
    zeros = jnp.zeros((d, HEAD_DIM), w_in.dtype)
    wsb = jnp.concatenate([cols["q_sb"] * (scale * LOG2E), cols["k_sb"], cols["v_sb"]],
                          axis=1).astype(BF16)
    wrot = jnp.concatenate([cols["q_d"] * (scale * LOG2E), cols["q_i"] * scale,
                            cols["k_d"], zeros, zeros, cols["k_d"],
                            cols["k_i"], zeros, zeros, cols["k_i"]], axis=1).astype(BF16)
    wt = jnp.concatenate([cols["v_d"], cols["w_i"]], axis=1).T.astype(BF16)
    wgate = jnp.concatenate([cols["g_sb"], cols["g_dsa"]], axis=1).astype(BF16)

    half = ROPE_DIM // 2
    invf = (ROPE_THETA ** (-jnp.arange(0, ROPE_DIM, 2, dtype=F32) / ROPE_DIM)).reshape(half, 1)
    lane = np.arange(LANES) % HEAD_DIM
    hit = (lane[None, :] % half == np.arange(half)[:, None])
    zero = np.zeros((half, LANES), np.float32)
    cos_rows = np.concatenate([hit & (lane < ROPE_DIM), zero, zero], axis=1)
    sin_rows = np.concatenate([zero, -1.0 * (hit & (lane < half)),
                               hit & (lane >= half) & (lane < ROPE_DIM)], axis=1)
    spread = jnp.asarray(np.concatenate([cos_rows, sin_rows] * 2, axis=0), BF16)
    base = jnp.asarray((lane >= ROPE_DIM).astype(np.float32).reshape(1, LANES))

    def tok(w):
        return pl.BlockSpec((tm, w), lambda i: (i, 0))

    def tok_t(r):
        return pl.BlockSpec((r, tm), lambda i: (0, i))

    outs = pl.pallas_call(
        _proj_kernel,
        grid=(n // tm,),
        in_specs=[tok(d), _full((1, d)), tok_t(1), _full(invf.shape), _full(spread.shape),
                  _full(base.shape),
                  _full(wsb.shape), _full(wrot.shape), _full(wt.shape)],
        out_specs=[tok(3 * wb), tok(wb), tok(wb), tok(2 * LANES), tok(2 * LANES),
                   pl.BlockSpec((tm // LANES, HEAD_DIM, LANES), lambda i: (i, 0, 0)),
                   tok_t(N_HEADS)],
        out_shape=[jax.ShapeDtypeStruct((n, 3 * wb), BF16),
                   jax.ShapeDtypeStruct((n, wb), BF16),
                   jax.ShapeDtypeStruct((n, wb), BF16),
                   jax.ShapeDtypeStruct((n, 2 * LANES), BF16),
                   jax.ShapeDtypeStruct((n, 2 * LANES), BF16),
                   jax.ShapeDtypeStruct((n // LANES, HEAD_DIM, LANES), BF16),
                   jax.ShapeDtypeStruct((N_HEADS, n), F32)],
        compiler_params=_params(1),
        name="mixer_proj",
    )(hf, norm.reshape(1, d), positions.reshape(1, n), invf, spread, base, wsb, wrot, wt)
    return outs, wgate


def _sb_kernel(q_ref, k_ref, v_ref, o_ref, qm_ref, acc_ref, car_ref, *, tq, nb):
    i = pl.program_id(1)
    pairs = W_BRANCH // LANES
    lane = lax.broadcasted_iota(I32, (tq, LANES), 1)
    lo_half = lane < HEAD_DIM
    for bb in range(nb):
        q = q_ref[bb]
        for g in range(pairs):
            qg = q[:, g * LANES:(g + 1) * LANES]
            qm_ref[bb * N_HEADS + 2 * g] = jnp.where(lo_half, qg, jnp.zeros_like(qg))
            qm_ref[bb * N_HEADS + 2 * g + 1] = jnp.where(lo_half, jnp.zeros_like(qg), qg)
    suffix = jnp.where(lax.broadcasted_iota(I32, (2 * tq, tq), 0) % tq
                       >= lax.broadcasted_iota(I32, (2 * tq, tq), 1), 1.0, 0.0).astype(BF16)
    strict = (lax.broadcasted_iota(I32, (tq, tq), 1) < lax.broadcasted_iota(I32, (tq, tq), 0))

    acc_ref[...] = jnp.zeros(acc_ref.shape, F32)
    car_ref[...] = jnp.zeros(car_ref.shape, F32)

    def group(j, diag):
        rows = pl.ds(pl.multiple_of(j * tq, tq), tq)
        hl, zs = [], []
        for bb in range(nb):
            for g in range(pairs):
                kg = k_ref[bb, rows, g * LANES:(g + 1) * LANES]
                qq = jnp.concatenate([qm_ref[bb * N_HEADS + 2 * g],
                                      qm_ref[bb * N_HEADS + 2 * g + 1]], axis=0)
                zz = lax.dot_general(qq, kg, _NT, preferred_element_type=F32)
                for hh in range(2):
                    z = zz[hh * tq:(hh + 1) * tq]
                    sp = jnp.maximum(z, 0.0) + jnp.log(1.0 + jnp.exp2(-jnp.abs(z))) * LOG2E
                    if diag:
                        sp = jnp.where(strict, sp, 0.0)
                    hi = sp.astype(BF16)
                    lo = (sp - hi.astype(F32)).astype(BF16)
                    hl.append(jnp.concatenate([hi, lo], axis=1))
                    zs.append(z)
        sums = jnp.dot(jnp.concatenate(hl, axis=0), suffix, preferred_element_type=F32)
        for bb in range(nb):
            for g in range(pairs):
                probs = []
                for hh in range(2):
                    h = bb * N_HEADS + 2 * g + hh
                    car = car_ref[h]
                    blk = sums[h * tq:(h + 1) * tq]
                    for l in range(tq // LANES):
                        sl = slice(l * LANES, (l + 1) * LANES)
                        a = jnp.exp2(zs[h][:, sl] - (blk[:, sl] + car))
                        if diag:
                            a = jnp.where(strict[:, sl], a, 0.0)
                        probs.append(a.astype(BF16))
                    car_ref[h] = car + jnp.broadcast_to(blk[:, 0:1], (tq, LANES))
                vg = v_ref[bb, rows, g * LANES:(g + 1) * LANES]
                vv = jnp.concatenate([jnp.where(lo_half, vg, jnp.zeros_like(vg)),
                                      jnp.where(lo_half, jnp.zeros_like(vg), vg)], axis=0)
                acc_ref[bb * pairs + g] += jnp.dot(jnp.concatenate(probs, axis=1), vv,
                                                   preferred_element_type=F32)

    group(i, True)

    def body(jj, carry):
        group(i - 1 - jj, False)
        return carry

    lax.fori_loop(0, i, body, 0)
    for bb in range(nb):
        for g in range(pairs):
            o_ref[bb, :, g * LANES:(g + 1) * LANES] = acc_ref[bb * pairs + g].astype(BF16)


def _sb_attention(qkv, b, s):
    sub = ATT_BLOCK
    tq = 2 * sub if s % (2 * sub) == 0 else sub
    nb = 2 if b % 2 == 0 else 1
    wb = W_BRANCH
    return pl.pallas_call(
        functools.partial(_sb_kernel, tq=tq, nb=nb),
        grid=(b // nb, s // tq),
        in_specs=[pl.BlockSpec((nb, tq, wb), lambda bi, i: (bi, i, 0)),
                  pl.BlockSpec((nb, s, wb), lambda bi, i: (bi, 0, 1)),
                  pl.BlockSpec((nb, s, wb), lambda bi, i: (bi, 0, 2))],
        out_specs=pl.BlockSpec((nb, tq, wb), lambda bi, i: (bi, i, 0)),
        out_shape=jax.ShapeDtypeStruct((b, s, wb), BF16),
        scratch_shapes=[pltpu.VMEM((nb * N_HEADS, tq, LANES), BF16),
                        pltpu.VMEM((nb * wb // LANES, tq, LANES), F32),
                        pltpu.VMEM((nb * N_HEADS, tq, LANES), F32)],
        compiler_params=_params(2),
        name="sb_attention",
    )(qkv, qkv, qkv)


def _dsa_kernel(qi_ref, qd_ref, ki_ref, kd_ref, vt_ref, wgt_ref, o_ref,
                sc_ref, fl_ref, bias_ref, lg_ref, num_ref, *, tb, span, n_sel):
    i = pl.program_id(1)
    nspan = ((i + 1) * tb - 1) // span + 1
    srow = lax.broadcasted_iota(I32, (span, tb), 0)
    scol = lax.broadcasted_iota(I32, (span, tb), 1)

    def rows(c):
        return pl.ds(pl.multiple_of(c * span, span), span)

    def span_causal(c):
        return (c * span + srow) <= (i * tb + scol)

    def fold(x, op):
        return op(x.reshape(x.shape[0] // SUBLANES, SUBLANES, x.shape[1]), axis=0)

    def pair_lhs(ref, c):
        kab = ref[0, rows(c), :]
        n16 = span // PACKED_SUBLANES
        return jnp.stack([kab[:, :LANES].reshape(n16, PACKED_SUBLANES, LANES),
                          kab[:, LANES:].reshape(n16, PACKED_SUBLANES, LANES)],
                         axis=1).reshape(2 * span, LANES)

    def head_dots(lhs, q_ref, gg):
        qq = jnp.concatenate([q_ref[0, :, (2 * gg) * LANES:(2 * gg + 1) * LANES],
                              q_ref[0, :, (2 * gg + 1) * LANES:(2 * gg + 2) * LANES]], axis=0)
        zz = lax.dot_general(lhs, qq, _NT, preferred_element_type=F32)
        zz = zz.reshape(span // PACKED_SUBLANES, 2, PACKED_SUBLANES, 2 * tb)
        even, odd = zz[:, 0].reshape(span, 2 * tb), zz[:, 1].reshape(span, 2 * tb)
        return [even[:, :tb], odd[:, :tb], even[:, tb:], odd[:, tb:]]

    def score_span(c, carry):
        lhs = pair_lhs(ki_ref, c)
        score = jnp.zeros((span, tb), F32)
        for gg in range(N_HEADS // 4):
            for hh, dots in enumerate(head_dots(lhs, qi_ref, gg)):
                h = 4 * gg + hh
                score = score + wgt_ref[h:h + 1, :] * jnp.maximum(dots, 0.0)
        score = jnp.where(span_causal(c), score, -jnp.inf)
        sc_ref[rows(c), :] = score
        near = score.astype(BF16).astype(F32)
        bits = lax.bitcast_convert_type(near, I32)
        below = lax.bitcast_convert_type(bits + jnp.where(bits >= 0, -65536, 65536), F32)
        fl_ref[rows(c), :] = jnp.where(near > score, below, near).astype(BF16)
        return carry

    lax.fori_loop(0, nspan, score_span, 0)

    def as_float(key):
        return lax.bitcast_convert_type(key ^ ((key >> 31) & np.int32(0x7FFFFFFF)), F32)

    def as_bf16(key_hi):
        bits = key_hi ^ ((key_hi >> 15) & np.int32(0x7FFF))
        return lax.bitcast_convert_type(
            jnp.broadcast_to(bits, (PACKED_SUBLANES, tb)).astype(I16), BF16)

    def count_ge_bf16(cand):
        n_acc = 8

        def scan(c, accs):
            fl = fl_ref[rows(c), :].reshape(span // PACKED_SUBLANES, PACKED_SUBLANES, tb)
            accs = list(accs)
            for r in range(span // PACKED_SUBLANES):
                a = accs[r % n_acc]
                accs[r % n_acc] = jnp.where(fl[r] >= cand, a + np.int16(1), a)
            return tuple(accs)

        accs = lax.fori_loop(0, nspan, scan, (jnp.zeros((PACKED_SUBLANES, tb), I16),) * n_acc)
        return jnp.sum(sum(accs[1:], accs[0]).astype(I32), axis=0, keepdims=True)

    def refine_hi(key_hi, state):
        thr, cnt = state
        c = count_ge_bf16(as_bf16(key_hi))
        ok = c >= n_sel
        return jnp.where(ok, key_hi, thr), jnp.where(ok, c, cnt)

    def count_ge(cand):
        n_acc = 8

        def scan(c, accs):
            sc = sc_ref[rows(c), :].reshape(span // SUBLANES, SUBLANES, tb)
            accs = list(accs)
            for r in range(span // SUBLANES):
                a = accs[r % n_acc]
                accs[r % n_acc] = jnp.where(sc[r] >= cand, a + 1, a)
            return tuple(accs)

        accs = lax.fori_loop(0, nspan, scan, (jnp.zeros((SUBLANES, tb), I32),) * n_acc)
        return jnp.sum(sum(accs[1:], accs[0]), axis=0, keepdims=True)

    def refine(key, state):
        thr, cnt = state
        c = count_ge(as_float(key))
        ok = c >= n_sel
        return jnp.where(ok, key, thr), jnp.where(ok, c, cnt)

    state = (jnp.full((1, tb), INT16_MIN, I32), jnp.zeros((1, tb), I32) + nspan * span)
    state = refine_hi(jnp.zeros((1, tb), I32), state)
    thr_hi, cnt = lax.fori_loop(
        0, 15, lambda it, st: refine_hi(st[0] + jnp.left_shift(np.int32(1), 14 - it), st), state)
    base = jnp.where(thr_hi == INT16_MIN, INT_MIN,
                     thr_hi * 65536 + jnp.where(thr_hi < 0, 65535, 0))
    thr_key, cnt = lax.fori_loop(
        0, 16, lambda it, st: refine(st[0] + jnp.left_shift(np.int32(1), 15 - it), st),
        (base, cnt))
    thr = jnp.where(thr_key == INT_MIN, -jnp.inf, as_float(thr_key))

    def plain_bias(c, carry):
        sel = (sc_ref[rows(c), :] >= thr) & span_causal(c)
        bias_ref[rows(c), :] = jnp.where(sel, 0.0, -jnp.inf)
        return carry

    lax.fori_loop(0, nspan, plain_bias, 0)

    tie = (cnt > n_sel) & (thr_key > INT_MIN)

    @pl.when(jnp.max(tie.astype(I32)) > 0)
    def _():
        def count_gt(c, acc):
            return acc + fold(jnp.where(sc_ref[rows(c), :] > thr, 1, 0).astype(I32), jnp.sum)
        n_gt = jnp.sum(lax.fori_loop(0, nspan, count_gt, jnp.zeros((SUBLANES, tb), I32)),
                       axis=0, keepdims=True)
        need = jnp.where(tie, (n_sel - n_gt).astype(F32), 3.0e38)
        before = jnp.where(lax.broadcasted_iota(I32, (span, span), 1)
                           < lax.broadcasted_iota(I32, (span, span), 0),
                           1.0, 0.0).astype(BF16)

        def tie_bias(c, seen):
            sc = sc_ref[rows(c), :]
            eq = sc == thr
            eqf = jnp.where(eq, 1.0, 0.0)
            rank = jnp.dot(before, eqf.astype(BF16), preferred_element_type=F32) + seen
            sel = ((sc > thr) | (eq & (rank < need))) & span_causal(c)
            bias_ref[rows(c), :] = jnp.where(sel, 0.0, -jnp.inf)
            return seen + jnp.sum(eqf, axis=0, keepdims=True)

        lax.fori_loop(0, nspan, tie_bias, jnp.zeros((1, tb), F32))

    ninf = jnp.full((SUBLANES, tb), -jnp.inf, F32)
    for gg in range(N_HEADS // 4):
        def logits_span(c, mx):
            bias = bias_ref[rows(c), :]
            new = []
            for h, dots in enumerate(head_dots(pair_lhs(kd_ref, c), qd_ref, gg)):
                lg = dots + bias
                lg_ref[h, rows(c), :] = lg
                new.append(jnp.maximum(mx[h], fold(lg, jnp.max)))
            return tuple(new)

        mx = lax.fori_loop(0, nspan, logits_span, (ninf,) * 4)
        mx = tuple(jnp.max(m, axis=0, keepdims=True) for m in mx)

        num_ref[...] = jnp.zeros(num_ref.shape, F32)

        def pv_span(c, carry):
            vt = jnp.concatenate([vt_ref[0, c * (span // LANES) + k]
                                  for k in range(span // LANES)], axis=1)
            vt1 = jnp.concatenate([vt, jnp.ones((PACKED_SUBLANES, span), BF16)], axis=0)
            for h in range(4):
                p = jnp.exp2(lg_ref[h, rows(c), :] - mx[h])
                num_ref[h] += jnp.dot(vt1, p.astype(BF16), preferred_element_type=F32)
            return carry

        lax.fori_loop(0, nspan, pv_span, 0)
        outs = [num_ref[h, :HEAD_DIM] / num_ref[h, HEAD_DIM:HEAD_DIM + 1] for h in range(4)]
        for g in range(2):
            pair = jnp.concatenate([outs[2 * g], outs[2 * g + 1]], axis=0)
            o_ref[0, :, (2 * gg + g) * LANES:(2 * gg + g + 1) * LANES] = pair.T.astype(BF16)


def _dsa_attention(qd, qi, kd, ki, vt, wgt, b, s):
    tb = DSA_QUERIES
    span = DSA_SPAN
    assert s % span == 0 and span % tb == 0, s
    nq = s // tb
    n_sel = min(TOPK_MAX, s // 4)
    return pl.pallas_call(
        functools.partial(_dsa_kernel, tb=tb, span=span, n_sel=n_sel),
        grid=(b, nq),
        in_specs=[pl.BlockSpec((1, tb, W_BRANCH), lambda bi, i: (bi, i, 0)),
                  pl.BlockSpec((1, tb, W_BRANCH), lambda bi, i: (bi, i, 0)),
                  pl.BlockSpec((1, s, 2 * LANES), lambda bi, i: (bi, 0, 0)),
                  pl.BlockSpec((1, s, 2 * LANES), lambda bi, i: (bi, 0, 0)),
                  pl.BlockSpec((1, s // LANES, HEAD_DIM, LANES), lambda bi, i: (bi, 0, 0, 0)),
                  pl.BlockSpec((N_HEADS, tb), lambda bi, i: (0, bi * nq + i))],
        out_specs=pl.BlockSpec((1, tb, W_BRANCH), lambda bi, i: (bi, i, 0)),
        out_shape=jax.ShapeDtypeStruct((b, s, W_BRANCH), BF16),
        scratch_shapes=[pltpu.VMEM((s, tb), F32), pltpu.VMEM((s, tb), BF16),
                        pltpu.VMEM((s, tb), F32),
                        pltpu.VMEM((4, s, tb), F32),
                        pltpu.VMEM((4, HEAD_DIM + PACKED_SUBLANES, tb), F32)],
        compiler_params=_params(2),
        name="dsa_attention",
    )(qi, qd, ki, kd, vt, wgt)


def _merge_kernel(h_ref, g_ref, ysb_ref, yds_ref, wg_ref, wosb_ref, wods_ref, wo_ref, o_ref):
    h = h_ref[...]
    d = h.shape[1]
    u = _rmsnorm(h, g_ref[...]).astype(BF16)
    gates = jax.nn.sigmoid(jnp.dot(u, wg_ref[...], preferred_element_type=F32))
    y_sb = jnp.dot(ysb_ref[...], wosb_ref[...], preferred_element_type=F32)
    y_ds = jnp.dot(yds_ref[...], wods_ref[...], preferred_element_type=F32)
    merged = gates[:, :d] * y_sb + gates[:, d:] * y_ds
    o_ref[...] = h + jnp.dot(merged.astype(BF16), wo_ref[...], preferred_element_type=F32)


def _merge_out(hf, norm, y_sb, y_dsa, wgate, w_out_sb, w_out_dsa, w_out):
    n, d = hf.shape
    tm = _token_tile(n, 1024)
    wb = W_BRANCH

    def tok(w):
        return pl.BlockSpec((tm, w), lambda i: (i, 0))

    return pl.pallas_call(
        _merge_kernel,
        grid=(n // tm,),
        in_specs=[tok(d), _full((1, d)), tok(wb), tok(wb), _full(wgate.shape),
                  _full((wb, d)), _full((wb, d)), _full((d, d))],
        out_specs=tok(d),
        out_shape=jax.ShapeDtypeStruct((n, d), F32),
        compiler_params=_params(1),
        name="merge_out",
    )(hf, norm.reshape(1, d), y_sb, y_dsa, wgate,
      w_out_sb.astype(BF16), w_out_dsa.astype(BF16), w_out.astype(BF16))


def kernel(x, p, positions, ffn1_norm, ffn1_w1, ffn1_w2, mix_norm, w_in, w_out_sb, w_out_dsa,
           w_out, ffn2_norm, ffn2_w1, ffn2_w2, ple_norm, ple_w_gate, ple_w_proj, final_norm):
    b, s, d = x.shape
    n = b * s
    depth = p.shape[0]
    nq = s // ATT_BLOCK
    h = x.reshape(n, d)
    for i in range(depth):
        h = _ffn_half_step(h, ffn1_norm[i], ffn1_w1[i], ffn1_w2[i])
        (sb, qd, qi, kd, ki, vt, wgt), wgate = _mixer_proj(h, positions, mix_norm[i], w_in[i])
        y_sb = _sb_attention(sb.reshape(b, s, 3 * W_BRANCH), b, s)
        vt4 = vt.reshape(b, nq, HEAD_DIM, ATT_BLOCK)
        y_dsa = _dsa_attention(qd.reshape(b, s, W_BRANCH), qi.reshape(b, s, W_BRANCH),
                               kd.reshape(b, s, 2 * LANES), ki.reshape(b, s, 2 * LANES),
                               vt4, wgt, b, s)
        h = _merge_out(h, mix_norm[i], y_sb.reshape(n, W_BRANCH), y_dsa.reshape(n, W_BRANCH),
                       wgate, w_out_sb[i], w_out_dsa[i], w_out[i])
        h = _ffn_ple_step(h, ffn2_norm[i], ffn2_w1[i], ffn2_w2[i], ple_norm[i],
                          p[i].reshape(n, -1), ple_w_gate[i], ple_w_proj[i], final_norm,
                          final=(i == depth - 1))
    return h.reshape(b, s, d)
```

```python
import functools

import numpy as np
import jax
import jax.numpy as jnp
from jax import lax
from jax.experimental import pallas as pl
from jax.experimental.pallas import tpu as pltpu

F32 = jnp.float32
BF16 = jnp.bfloat16
I32 = jnp.int32
I16 = jnp.int16

HEAD_DIM = 64
N_HEADS = 8
W_BRANCH = N_HEADS * HEAD_DIM
TOPK_MAX = 256
ROPE_THETA = 500000.0
ROPE_DIM = HEAD_DIM // 4
EPS = 1e-6
LOG2E = 1.4426950408889634

LANES = 128
SUBLANES = 8
ATT_BLOCK = 128
DSA_QUERIES = 512
DSA_SPAN = 512
PACKED_SUBLANES = 16
INT_MIN = np.int32(-2**31)
INT16_MIN = -2**15
VMEM_LIMIT = 56 * 1024 * 1024

_NT = (((1,), (1,)), ((), ()))


def _rmsnorm(x, g):
    ms = jnp.mean(x * x, axis=-1, keepdims=True)
    return x * lax.rsqrt(ms + EPS) * g


def _token_tile(n, largest=512):
    for tm in (largest, 512, 256, 128):
        if n % tm == 0:
            return tm
    raise ValueError(f"token count {n} must be a multiple of 128")


def _ff_chunks(d_ff):
    assert d_ff % 256 == 0, d_ff
    chunks, left = [], d_ff
    while left:
        c = min(768, left)
        chunks.append(c)
        left -= c
    return tuple(chunks)


def _params(n_axes):
    return pltpu.CompilerParams(dimension_semantics=("arbitrary",) * n_axes,
                                vmem_limit_bytes=VMEM_LIMIT)


def _full(shape):
    nd = len(shape)
    return pl.BlockSpec(shape, lambda *_: (0,) * nd)


def _swiglu_half_step(x, g_ref, w1a_ref, w1b_ref, w2_ref, chunks):
    xn = _rmsnorm(x, g_ref[...]).astype(BF16)
    acc = jnp.zeros(x.shape, F32)
    off = 0
    for cw in chunks:
        a = jnp.dot(xn, w1a_ref[:, off:off + cw], preferred_element_type=F32)
        b = jnp.dot(xn, w1b_ref[:, off:off + cw], preferred_element_type=F32)
        hid = (a * jax.nn.sigmoid(a) * b).astype(BF16)
        acc = acc + jnp.dot(hid, w2_ref[off:off + cw, :], preferred_element_type=F32)
        off += cw
    return x + 0.5 * acc


def _ffn_ple_kernel(x_ref, g_ref, w1a_ref, w1b_ref, w2_ref, gp_ref, p_ref, wg_ref, wp_ref,
                    gf_ref, o_ref, *, chunks, final):
    h = _swiglu_half_step(x_ref[...], g_ref, w1a_ref, w1b_ref, w2_ref, chunks)
    u = _rmsnorm(h, gp_ref[...]).astype(BF16)
    gate = jax.nn.sigmoid(jnp.dot(u, wg_ref[...], preferred_element_type=F32))
    emb = jnp.dot(p_ref[...].astype(BF16), wp_ref[...], preferred_element_type=F32)
    h = h + gate * emb
    o_ref[...] = _rmsnorm(h, gf_ref[...]) if final else h


def _ffn_weights(w1, w2):
    d, d_ff = w1.shape[0], w2.shape[0]
    w1b = w1.astype(BF16)
    halves = [pl.BlockSpec((d, d_ff), functools.partial(lambda j, *_: (0, j), j),
                           pipeline_mode=pl.Buffered(1)) for j in range(2)]
    return (w1b, w1b, w2.astype(BF16)), halves + [_full((d_ff, d))]


def _ffn_ple_step(xf, norm, w1, w2, ple_norm, pf, ple_w_gate, ple_w_proj, final_norm, final):
    n, d = xf.shape
    dp = pf.shape[1]
    tm = _token_tile(n)
    weights, wspecs = _ffn_weights(w1, w2)

    def tok(w):
        return pl.BlockSpec((tm, w), lambda i: (i, 0))

    return pl.pallas_call(
        functools.partial(_ffn_ple_kernel, chunks=_ff_chunks(w2.shape[0]), final=final),
        grid=(n // tm,),
        in_specs=[tok(d), _full((1, d))] + wspecs
        + [_full((1, d)), tok(dp), _full((d, d)), _full((dp, d)), _full((1, d))],
        out_specs=tok(d),
        out_shape=jax.ShapeDtypeStruct((n, d), F32),
        compiler_params=_params(1),
        name="ffn_ple_step",
    )(xf, norm.reshape(1, d), *weights, ple_norm.reshape(1, d), pf,
      ple_w_gate.astype(BF16), ple_w_proj.astype(BF16), final_norm.reshape(1, d))


def _ffn_proj_kernel(x_ref, g1_ref, w1a_ref, w1b_ref, w2_ref, g_ref, pos_ref, invf_ref, spread_ref,
                     base_ref, wsb_ref, wrot_ref, wt_ref,
                     h_ref, sb_ref, qd_ref, qi_ref, kd_ref, ki_ref, vt_ref, wgt_ref, *, chunks):
    h = _swiglu_half_step(x_ref[...], g1_ref, w1a_ref, w1b_ref, w2_ref, chunks)
    h_ref[...] = h
    u = _rmsnorm(h, g_ref[...]).astype(BF16)
    sb_ref[...] = jnp.dot(u, wsb_ref[...], preferred_element_type=F32).astype(BF16)

    ang = invf_ref[...] * pos_ref[...].astype(F32)
    cs = jnp.concatenate([jnp.cos(ang), jnp.sin(ang)], axis=0)
    cs_hi = cs.astype(BF16)
    cs_lo = (cs - cs_hi.astype(F32)).astype(BF16)
    pat = lax.dot_general(jnp.concatenate([cs_hi, cs_lo], axis=0), spread_ref[...],
                          (((0,), (0,)), ((), ())), preferred_element_type=F32)
    half = ROPE_DIM // 2
    cos = pat[:, :LANES] + base_ref[...]
    s_lo = pat[:, LANES:2 * LANES]
    s_hi = pat[:, 2 * LANES:]
    rot = jnp.dot(u, wrot_ref[...], preferred_element_type=F32)
    outs = ((qd_ref, 0, W_BRANCH), (qi_ref, W_BRANCH, W_BRANCH),
            (kd_ref, 2 * W_BRANCH, 2 * LANES), (ki_ref, 2 * W_BRANCH + 2 * LANES, 2 * LANES))
    for ref, base, width in outs:
        for g in range(width // LANES):
            xg = rot[:, base + g * LANES: base + (g + 1) * LANES]
            yg = (xg * cos + pltpu.roll(xg, LANES - half, 1) * s_lo
                  + pltpu.roll(xg, half, 1) * s_hi)
            ref[:, g * LANES:(g + 1) * LANES] = yg.astype(BF16)

    tr = lax.dot_general(wt_ref[...], u, _NT, preferred_element_type=F32)
    for kb in range(vt_ref.shape[0]):
        vt_ref[kb] = tr[:HEAD_DIM, kb * LANES:(kb + 1) * LANES].astype(BF16)
    wgt_ref[...] = tr[HEAD_DIM:] * (N_HEADS ** -0.5)


def _ffn_mixer_proj(xf, ffn_norm, w1, w2, positions, norm, w_in):
    n, d = xf.shape
    tm = _token_tile(n)
    weights, wspecs = _ffn_weights(w1, w2)
    wb = W_BRANCH
    o = 0
    cols = {}
    for name, size in (("q_sb", wb), ("k_sb", wb), ("v_sb", wb), ("q_d", wb), ("k_d", HEAD_DIM),
                       ("v_d", HEAD_DIM), ("q_i", wb), ("k_i", HEAD_DIM), ("w_i", N_HEADS),
                       ("g_sb", d), ("g_dsa", d)):
        cols[name] = w_in[:, o:o + size]
        o += size
    scale = HEAD_DIM ** -0.5
    zeros = jnp.zeros((d, HEAD_DIM), w_in.dtype)
    wsb = jnp.concatenate([cols["q_sb"] * (scale * LOG2E), cols["k_sb"], cols["v_sb"]],
                          axis=1).astype(BF16)
    wrot = jnp.concatenate([cols["q_d"] * (scale * LOG2E), cols["q_i"] * scale,
                            cols["k_d"], zeros, zeros, cols["k_d"],
                            cols["k_i"], zeros, zeros, cols["k_i"]], axis=1).astype(BF16)
    wt = jnp.concatenate([cols["v_d"], cols["w_i"]], axis=1).T.astype(BF16)
    wgate = jnp.concatenate([cols["g_sb"], cols["g_dsa"]], axis=1).astype(BF16)

    half = ROPE_DIM // 2
    invf = (ROPE_THETA ** (-jnp.arange(0, ROPE_DIM, 2, dtype=F32) / ROPE_DIM)).reshape(half, 1)
    lane = np.arange(LANES) % HEAD_DIM
    hit = (lane[None, :] % half == np.arange(half)[:, None])
    zero = np.zeros((half, LANES), np.float32)
    cos_rows = np.concatenate([hit & (lane < ROPE_DIM), zero, zero], axis=1)
    sin_rows = np.concatenate([zero, -1.0 * (hit & (lane < half)),
                               hit & (lane >= half) & (lane < ROPE_DIM)], axis=1)
    spread = jnp.asarray(np.concatenate([cos_rows, sin_rows] * 2, axis=0), BF16)
    base = jnp.asarray((lane >= ROPE_DIM).astype(np.float32).reshape(1, LANES))

    def tok(w):
        return pl.BlockSpec((tm, w), lambda i: (i, 0))

    def tok_t(r):
        return pl.BlockSpec((r, tm), lambda i: (0, i))

    outs = pl.pallas_call(
        functools.partial(_ffn_proj_kernel, chunks=_ff_chunks(w2.shape[0])),
        grid=(n // tm,),
        in_specs=[tok(d), _full((1, d))] + wspecs
        + [_full((1, d)), tok_t(1), _full(invf.shape), _full(spread.shape), _full(base.shape),
           _full(wsb.shape), _full(wrot.shape), _full(wt.shape)],
        out_specs=[tok(d), tok(3 * wb), tok(wb), tok(wb), tok(2 * LANES), tok(2 * LANES),
                   pl.BlockSpec((tm // LANES, HEAD_DIM, LANES), lambda i: (i, 0, 0)),
                   tok_t(N_HEADS)],
        out_shape=[jax.ShapeDtypeStruct((n, d), F32),
                   jax.ShapeDtypeStruct((n, 3 * wb), BF16),
                   jax.ShapeDtypeStruct((n, wb), BF16),
                   jax.ShapeDtypeStruct((n, wb), BF16),
                   jax.ShapeDtypeStruct((n, 2 * LANES), BF16),
                   jax.ShapeDtypeStruct((n, 2 * LANES), BF16),
                   jax.ShapeDtypeStruct((n // LANES, HEAD_DIM, LANES), BF16),
                   jax.ShapeDtypeStruct((N_HEADS, n), F32)],
        compiler_params=_params(1),
        name="ffn_mixer_proj",
    )(xf, ffn_norm.reshape(1, d), *weights, norm.reshape(1, d), positions.reshape(1, n),
      invf, spread, base, wsb, wrot, wt)
    return outs, wgate


def _sb_kernel(q_ref, k_ref, v_ref, o_ref, qm_ref, acc_ref, car_ref, *, tq, nb):
    i = pl.program_id(1)
    pairs = W_BRANCH // LANES
    lane = lax.broadcasted_iota(I32, (tq, LANES), 1)
    lo_half = lane < HEAD_DIM
    for bb in range(nb):
        q = q_ref[bb]
        for g in range(pairs):
            qg = q[:, g * LANES:(g + 1) * LANES]
            qm_ref[bb * N_HEADS + 2 * g] = jnp.where(lo_half, qg, jnp.zeros_like(qg))
            qm_ref[bb * N_HEADS + 2 * g + 1] = jnp.where(lo_half, jnp.zeros_like(qg), qg)
    suffix = jnp.where(lax.broadcasted_iota(I32, (2 * tq, tq), 0) % tq
                       >= lax.broadcasted_iota(I32, (2 * tq, tq), 1), 1.0, 0.0).astype(BF16)
    strict = (lax.broadcasted_iota(I32, (tq, tq), 1) < lax.broadcasted_iota(I32, (tq, tq), 0))

    acc_ref[...] = jnp.zeros(acc_ref.shape, F32)
    car_ref[...] = jnp.zeros(car_ref.shape, F32)

    def group(j, diag):
        rows = pl.ds(pl.multiple_of(j * tq, tq), tq)
        hl, zs = [], []
        for bb in range(nb):
            for g in range(pairs):
                kg = k_ref[bb, rows, g * LANES:(g + 1) * LANES]
                qq = jnp.concatenate([qm_ref[bb * N_HEADS + 2 * g],
                                      qm_ref[bb * N_HEADS + 2 * g + 1]], axis=0)
                zz = lax.dot_general(qq, kg, _NT, preferred_element_type=F32)
                for hh in range(2):
                    z = zz[hh * tq:(hh + 1) * tq]
                    sp = jnp.maximum(z, 0.0) + jnp.log(1.0 + jnp.exp2(-jnp.abs(z))) * LOG2E
                    if diag:
                        sp = jnp.where(strict, sp, 0.0)
                    hi = sp.astype(BF16)
                    lo = (sp - hi.astype(F32)).astype(BF16)
                    hl.append(jnp.concatenate([hi, lo], axis=1))
                    zs.append(z)
        sums = jnp.dot(jnp.concatenate(hl, axis=0), suffix, preferred_element_type=F32)
        for bb in range(nb):
            for g in range(pairs):
                probs = []
                for hh in range(2):
                    h = bb * N_HEADS + 2 * g + hh
                    car = car_ref[h]
                    blk = sums[h * tq:(h + 1) * tq]
                    for l in range(tq // LANES):
                        sl = slice(l * LANES, (l + 1) * LANES)
                        a = jnp.exp2(zs[h][:, sl] - (blk[:, sl] + car))
                        if diag:
                            a = jnp.where(strict[:, sl], a, 0.0)
                        probs.append(a.astype(BF16))
                    car_ref[h] = car + jnp.broadcast_to(blk[:, 0:1], (tq, LANES))
                vg = v_ref[bb, rows, g * LANES:(g + 1) * LANES]
                vv = jnp.concatenate([jnp.where(lo_half, vg, jnp.zeros_like(vg)),
                                      jnp.where(lo_half, jnp.zeros_like(vg), vg)], axis=0)
                acc_ref[bb * pairs + g] += jnp.dot(jnp.concatenate(probs, axis=1), vv,
                                                   preferred_element_type=F32)

    group(i, True)

    def body(jj, carry):
        group(i - 1 - jj, False)
        return carry

    lax.fori_loop(0, i, body, 0)
    for bb in range(nb):
        for g in range(pairs):
            o_ref[bb, :, g * LANES:(g + 1) * LANES] = acc_ref[bb * pairs + g].astype(BF16)


def _sb_attention(qkv, b, s):
    sub = ATT_BLOCK
    tq = 2 * sub if s % (2 * sub) == 0 else sub
    nb = 2 if b % 2 == 0 else 1
    wb = W_BRANCH
    return pl.pallas_call(
        functools.partial(_sb_kernel, tq=tq, nb=nb),
        grid=(b // nb, s // tq),
        in_specs=[pl.BlockSpec((nb, tq, wb), lambda bi, i: (bi, i, 0)),
                  pl.BlockSpec((nb, s, wb), lambda bi, i: (bi, 0, 1)),
                  pl.BlockSpec((nb, s, wb), lambda bi, i: (bi, 0, 2))],
        out_specs=pl.BlockSpec((nb, tq, wb), lambda bi, i: (bi, i, 0)),
        out_shape=jax.ShapeDtypeStruct((b, s, wb), BF16),
        scratch_shapes=[pltpu.VMEM((nb * N_HEADS, tq, LANES), BF16),
                        pltpu.VMEM((nb * wb // LANES, tq, LANES), F32),
                        pltpu.VMEM((nb * N_HEADS, tq, LANES), F32)],
        compiler_params=_params(2),
        name="sb_attention",
    )(qkv, qkv, qkv)


def _dsa_kernel(qi_ref, qd_ref, ki_ref, kd_ref, vt_ref, wgt_ref, o_ref,
                sc_ref, fl_ref, bias_ref, lg_ref, num_ref, *, tb, span, n_sel):
    i = pl.program_id(1)
    nspan = ((i + 1) * tb - 1) // span + 1
    srow = lax.broadcasted_iota(I32, (span, tb), 0)
    scol = lax.broadcasted_iota(I32, (span, tb), 1)

    def rows(c):
        return pl.ds(pl.multiple_of(c * span, span), span)

    def span_causal(c):
        return (c * span + srow) <= (i * tb + scol)

    def fold(x, op):
        return op(x.reshape(x.shape[0] // SUBLANES, SUBLANES, x.shape[1]), axis=0)

    def pair_lhs(ref, c):
        kab = ref[0, rows(c), :]
        n16 = span // PACKED_SUBLANES
        return jnp.stack([kab[:, :LANES].reshape(n16, PACKED_SUBLANES, LANES),
                          kab[:, LANES:].reshape(n16, PACKED_SUBLANES, LANES)],
                         axis=1).reshape(2 * span, LANES)

    def head_dots(lhs, q_ref, gg):
        qq = jnp.concatenate([q_ref[0, :, (2 * gg) * LANES:(2 * gg + 1) * LANES],
                              q_ref[0, :, (2 * gg + 1) * LANES:(2 * gg + 2) * LANES]], axis=0)
        zz = lax.dot_general(lhs, qq, _NT, preferred_element_type=F32)
        zz = zz.reshape(span // PACKED_SUBLANES, 2, PACKED_SUBLANES, 2 * tb)
        even, odd = zz[:, 0].reshape(span, 2 * tb), zz[:, 1].reshape(span, 2 * tb)
        return [even[:, :tb], odd[:, :tb], even[:, tb:], odd[:, tb:]]

    def score_span(c, carry):
        lhs = pair_lhs(ki_ref, c)
        score = jnp.zeros((span, tb), F32)
        for gg in range(N_HEADS // 4):
            for hh, dots in enumerate(head_dots(lhs, qi_ref, gg)):
                h = 4 * gg + hh
                score = score + wgt_ref[h:h + 1, :] * jnp.maximum(dots, 0.0)
        score = jnp.where(span_causal(c), score, -jnp.inf)
        sc_ref[rows(c), :] = score
        near = score.astype(BF16).astype(F32)
        bits = lax.bitcast_convert_type(near, I32)
        below = lax.bitcast_convert_type(bits + jnp.where(bits >= 0, -65536, 65536), F32)
        fl_ref[rows(c), :] = jnp.where(near > score, below, near).astype(BF16)
        return carry

    lax.fori_loop(0, nspan, score_span, 0)

    def as_float(key):
        return lax.bitcast_convert_type(key ^ ((key >> 31) & np.int32(0x7FFFFFFF)), F32)

    def as_bf16(key_hi):
        bits = key_hi ^ ((key_hi >> 15) & np.int32(0x7FFF))
        return lax.bitcast_convert_type(
            jnp.broadcast_to(bits, (PACKED_SUBLANES, tb)).astype(I16), BF16)

    def count_ge_bf16(cand):
        n_acc = 8

        def scan(c, accs):
            fl = fl_ref[rows(c), :].reshape(span // PACKED_SUBLANES, PACKED_SUBLANES, tb)
            accs = list(accs)
            for r in range(span // PACKED_SUBLANES):
                a = accs[r % n_acc]
                accs[r % n_acc] = jnp.where(fl[r] >= cand, a + np.int16(1), a)
            return tuple(accs)

        accs = lax.fori_loop(0, nspan, scan, (jnp.zeros((PACKED_SUBLANES, tb), I16),) * n_acc)
        return jnp.sum(sum(accs[1:], accs[0]).astype(I32), axis=0, keepdims=True)

    def refine_hi(key_hi, state):
        thr, cnt = state
        c = count_ge_bf16(as_bf16(key_hi))
        ok = c >= n_sel
        return jnp.where(ok, key_hi, thr), jnp.where(ok, c, cnt)

    def count_ge(cand):
        n_acc = 8

        def scan(c, accs):
            sc = sc_ref[rows(c), :].reshape(span // SUBLANES, SUBLANES, tb)
            accs = list(accs)
            for r in range(span // SUBLANES):
                a = accs[r % n_acc]
                accs[r % n_acc] = jnp.where(sc[r] >= cand, a + 1, a)
            return tuple(accs)

        accs = lax.fori_loop(0, nspan, scan, (jnp.zeros((SUBLANES, tb), I32),) * n_acc)
        return jnp.sum(sum(accs[1:], accs[0]), axis=0, keepdims=True)

    def refine(key, state):
        thr, cnt = state
        c = count_ge(as_float(key))
        ok = c >= n_sel
        return jnp.where(ok, key, thr), jnp.where(ok, c, cnt)

    state = (jnp.full((1, tb), INT16_MIN, I32), jnp.zeros((1, tb), I32) + nspan * span)
    state = refine_hi(jnp.zeros((1, tb), I32), state)
    thr_hi, cnt = lax.fori_loop(
        0, 15, lambda it, st: refine_hi(st[0] + jnp.left_shift(np.int32(1), 14 - it), st), state)
    base = jnp.where(thr_hi == INT16_MIN, INT_MIN,
                     thr_hi * 65536 + jnp.where(thr_hi < 0, 65535, 0))
    thr_key, cnt = lax.fori_loop(
        0, 16, lambda it, st: refine(st[0] + jnp.left_shift(np.int32(1), 15 - it), st),
        (base, cnt))
    thr = jnp.where(thr_key == INT_MIN, -jnp.inf, as_float(thr_key))

    def plain_bias(c, carry):
        sel = (sc_ref[rows(c), :] >= thr) & span_causal(c)
        bias_ref[rows(c), :] = jnp.where(sel, 0.0, -jnp.inf)
        return carry

    lax.fori_loop(0, nspan, plain_bias, 0)

    tie = (cnt > n_sel) & (thr_key > INT_MIN)

    @pl.when(jnp.max(tie.astype(I32)) > 0)
    def _():
        def count_gt(c, acc):
            return acc + fold(jnp.where(sc_ref[rows(c), :] > thr, 1, 0).astype(I32), jnp.sum)
        n_gt = jnp.sum(lax.fori_loop(0, nspan, count_gt, jnp.zeros((SUBLANES, tb), I32)),
                       axis=0, keepdims=True)
        need = jnp.where(tie, (n_sel - n_gt).astype(F32), 3.0e38)
        before = jnp.where(lax.broadcasted_iota(I32, (span, span), 1)
                           < lax.broadcasted_iota(I32, (span, span), 0),
                           1.0, 0.0).astype(BF16)

        def tie_bias(c, seen):
            sc = sc_ref[rows(c), :]
            eq = sc == thr
            eqf = jnp.where(eq, 1.0, 0.0)
            rank = jnp.dot(before, eqf.astype(BF16), preferred_element_type=F32) + seen
            sel = ((sc > thr) | (eq & (rank < need))) & span_causal(c)
            bias_ref[rows(c), :] = jnp.where(sel, 0.0, -jnp.inf)
            return seen + jnp.sum(eqf, axis=0, keepdims=True)

        lax.fori_loop(0, nspan, tie_bias, jnp.zeros((1, tb), F32))

    ninf = jnp.full((SUBLANES, tb), -jnp.inf, F32)
    for gg in range(N_HEADS // 4):
        def logits_span(c, mx):
            bias = bias_ref[rows(c), :]
            new = []
            for h, dots in enumerate(head_dots(pair_lhs(kd_ref, c), qd_ref, gg)):
                lg = dots + bias
                lg_ref[h, rows(c), :] = lg
                new.append(jnp.maximum(mx[h], fold(lg, jnp.max)))
            return tuple(new)

        mx = lax.fori_loop(0, nspan, logits_span, (ninf,) * 4)
        mx = tuple(jnp.max(m, axis=0, keepdims=True) for m in mx)

        num_ref[...] = jnp.zeros(num_ref.shape, F32)

        def pv_span(c, carry):
            vt = jnp.concatenate([vt_ref[0, c * (span // LANES) + k]
                                  for k in range(span // LANES)], axis=1)
            vt1 = jnp.concatenate([vt, jnp.ones((PACKED_SUBLANES, span), BF16)], axis=0)
            for h in range(4):
                p = jnp.exp2(lg_ref[h, rows(c), :] - mx[h])
                num_ref[h] += jnp.dot(vt1, p.astype(BF16), preferred_element_type=F32)
            return carry

        lax.fori_loop(0, nspan, pv_span, 0)
        outs = [num_ref[h, :HEAD_DIM] / num_ref[h, HEAD_DIM:HEAD_DIM + 1] for h in range(4)]
        for g in range(2):
            pair = jnp.concatenate([outs[2 * g], outs[2 * g + 1]], axis=0)
            o_ref[0, :, (2 * gg + g) * LANES:(2 * gg + g + 1) * LANES] = pair.T.astype(BF16)


def _dsa_attention(qd, qi, kd, ki, vt, wgt, b, s):
    tb = DSA_QUERIES
    span = DSA_SPAN
    assert s % span == 0 and span % tb == 0, s
    nq = s // tb
    n_sel = min(TOPK_MAX, s // 4)
    return pl.pallas_call(
        functools.partial(_dsa_kernel, tb=tb, span=span, n_sel=n_sel),
        grid=(b, nq),
        in_specs=[pl.BlockSpec((1, tb, W_BRANCH), lambda bi, i: (bi, i, 0)),
                  pl.BlockSpec((1, tb, W_BRANCH), lambda bi, i: (bi, i, 0)),
                  pl.BlockSpec((1, s, 2 * LANES), lambda bi, i: (bi, 0, 0)),
                  pl.BlockSpec((1, s, 2 * LANES), lambda bi, i: (bi, 0, 0)),
                  pl.BlockSpec((1, s // LANES, HEAD_DIM, LANES), lambda bi, i: (bi, 0, 0, 0)),
                  pl.BlockSpec((N_HEADS, tb), lambda bi, i: (0, bi * nq + i))],
        out_specs=pl.BlockSpec((1, tb, W_BRANCH), lambda bi, i: (bi, i, 0)),
        out_shape=jax.ShapeDtypeStruct((b, s, W_BRANCH), BF16),
        scratch_shapes=[pltpu.VMEM((s, tb), F32), pltpu.VMEM((s, tb), BF16),
                        pltpu.VMEM((s, tb), F32),
                        pltpu.VMEM((4, s, tb), F32),
                        pltpu.VMEM((4, HEAD_DIM + PACKED_SUBLANES, tb), F32)],
        compiler_params=_params(2),
        name="dsa_attention",
    )(qi, qd, ki, kd, vt, wgt)


def _merge_kernel(h_ref, g_ref, ysb_ref, yds_ref, wg_ref, wosb_ref, wods_ref, wo_ref, o_ref):
    h = h_ref[...]
    d = h.shape[1]
    u = _rmsnorm(h, g_ref[...]).astype(BF16)
    gates = jax.nn.sigmoid(jnp.dot(u, wg_ref[...], preferred_element_type=F32))
    y_sb = jnp.dot(ysb_ref[...], wosb_ref[...], preferred_element_type=F32)
    y_ds = jnp.dot(yds_ref[...], wods_ref[...], preferred_element_type=F32)
    merged = gates[:, :d] * y_sb + gates[:, d:] * y_ds
    o_ref[...] = h + jnp.dot(merged.astype(BF16), wo_ref[...], preferred_element_type=F32)


def _merge_out(hf, norm, y_sb, y_dsa, wgate, w_out_sb, w_out_dsa, w_out):
    n, d = hf.shape
    tm = _token_tile(n, 1024)
    wb = W_BRANCH

    def tok(w):
        return pl.BlockSpec((tm, w), lambda i: (i, 0))

    return pl.pallas_call(
        _merge_kernel,
        grid=(n // tm,),
        in_specs=[tok(d), _full((1, d)), tok(wb), tok(wb), _full(wgate.shape),
                  _full((wb, d)), _full((wb, d)), _full((d, d))],
        out_specs=tok(d),
        out_shape=jax.ShapeDtypeStruct((n, d), F32),
        compiler_params=_params(1),
        name="merge_out",
    )(hf, norm.reshape(1, d), y_sb, y_dsa, wgate,
      w_out_sb.astype(BF16), w_out_dsa.astype(BF16), w_out.astype(BF16))


def kernel(x, p, positions, ffn1_norm, ffn1_w1, ffn1_w2, mix_norm, w_in, w_out_sb, w_out_dsa,
           w_out, ffn2_norm, ffn2_w1, ffn2_w2, ple_norm, ple_w_gate, ple_w_proj, final_norm):
    b, s, d = x.shape
    n = b * s
    depth = p.shape[0]
    nq = s // ATT_BLOCK
    h = x.reshape(n, d)
    for i in range(depth):
        (h, sb, qd, qi, kd, ki, vt, wgt), wgate = _ffn_mixer_proj(
            h, ffn1_norm[i], ffn1_w1[i], ffn1_w2[i], positions, mix_norm[i], w_in[i])
        y_sb = _sb_attention(sb.reshape(b, s, 3 * W_BRANCH), b, s)
        vt4 = vt.reshape(b, nq, HEAD_DIM, ATT_BLOCK)
        y_dsa = _dsa_attention(qd.reshape(b, s, W_BRANCH), qi.reshape(b, s, W_BRANCH),
                               kd.reshape(b, s, 2 * LANES), ki.reshape(b, s, 2 * LANES),
                               vt4, wgt, b, s)
        h = _merge_out(h, mix_norm[i], y_sb.reshape(n, W_BRANCH), y_dsa.reshape(n, W_BRANCH),
                       wgate, w_out_sb[i], w_out_dsa[i], w_out[i])
        h = _ffn_ple_step(h, ffn2_norm[i], ffn2_w1[i], ffn2_w2[i], ple_norm[i],
                          p[i].reshape(n, -1), ple_w_gate[i], ple_w_proj[i], final_norm,
                          final=(i == depth - 1))
    return h.reshape(b, s, d)
```
